```python
import jax, jax.numpy as jnp
from jax import lax
import numpy as np

D_MODEL = 1024
BATCH = 8
SEQ = 4096
DEPTH = 4

N_META = 16
EPS = 1e-6
SSD_D_INNER = 2 * D_MODEL
SSD_HEAD_DIM = 64
SSD_HEADS = SSD_D_INNER // SSD_HEAD_DIM
SSD_GROUPS = 8
SSD_HPG = SSD_HEADS // SSD_GROUPS
SSD_STATE = 128
SSD_CONV = 4
SSD_CHUNK = 128
SSD_CONV_DIM = SSD_D_INNER + 2 * SSD_GROUPS * SSD_STATE
SSD_IN_DIM = SSD_D_INNER + SSD_CONV_DIM + SSD_HEADS
MLA_HEADS = 16
MLA_NOPE = 64
MLA_ROPE = 32
MLA_V = 64
MLA_QK = MLA_NOPE + MLA_ROPE
MLA_Q_RANK = 384
MLA_KV_RANK = 256
MLA_IN_DIM = MLA_Q_RANK + MLA_KV_RANK + MLA_ROPE
ROPE_THETA = 10000.0
ATTN_BLOCK = 128
D_FF = 4 * D_MODEL
N_SSD_LAYERS = (DEPTH + 1) // 2
N_MLA_LAYERS = DEPTH // 2

kernel_name = "hybrid_ssd_mla_meta_trunk"


def rms_norm(x, gain):
    xf = x.astype(jnp.float32)
    y = xf * lax.rsqrt(jnp.mean(xf * xf, axis=-1, keepdims=True) + EPS)
    return (y * gain.astype(jnp.float32)).astype(x.dtype)


def causal_depthwise_conv(u, w, b):
    out = lax.conv_general_dilated(
        u, w[:, None, :].astype(u.dtype), window_strides=(1,),
        padding=[(SSD_CONV - 1, 0)], dimension_numbers=('NWC', 'WIO', 'NWC'),
        feature_group_count=u.shape[-1])
    return out + b.astype(u.dtype)


def ssd_mixer(h, w_in, conv_w, conv_b, dt_bias, a_log, d_skip, norm_g, w_out):
    f32 = jnp.float32
    bsz, L, _ = h.shape
    zxbcdt = h @ w_in
    z, xbc, dt = jnp.split(zxbcdt, [SSD_D_INNER, SSD_D_INNER + SSD_CONV_DIM], axis=-1)
    xbc = jax.nn.silu(causal_depthwise_conv(xbc, conv_w, conv_b))
    xs, b_in, c_in = jnp.split(xbc, [SSD_D_INNER, SSD_D_INNER + SSD_GROUPS * SSD_STATE], axis=-1)
    dt = jax.nn.softplus(dt.astype(f32) + dt_bias.astype(f32))
    a = -jnp.exp(a_log.astype(f32))

    pad = (-L) % SSD_CHUNK
    n_chunks = (L + pad) // SSD_CHUNK

    def front_pad(t):
        return jnp.pad(t.astype(f32), [(0, 0), (pad, 0)] + [(0, 0)] * (t.ndim - 2))

    x_c = front_pad(xs).reshape(bsz, n_chunks, SSD_CHUNK, SSD_GROUPS, SSD_HPG, SSD_HEAD_DIM)
    b_c = front_pad(b_in).reshape(bsz, n_chunks, SSD_CHUNK, SSD_GROUPS, SSD_STATE)
    c_c = front_pad(c_in).reshape(bsz, n_chunks, SSD_CHUNK, SSD_GROUPS, SSD_STATE)
    dt_c = front_pad(dt).reshape(bsz, n_chunks, SSD_CHUNK, SSD_GROUPS, SSD_HPG)
    xdt = x_c * dt_c[..., None]
    a_dt = (dt_c * a.reshape(SSD_GROUPS, SSD_HPG)).transpose(0, 1, 3, 4, 2)
    a_cs = jnp.cumsum(a_dt, axis=-1)

    idx = jnp.arange(SSD_CHUNK)
    causal = idx[:, None] >= idx[None, :]
    decay = jnp.exp(jnp.where(causal, a_cs[..., :, None] - a_cs[..., None, :], -jnp.inf))
    cb = jnp.einsum('bclgn,bcsgn->bcgls', c_c, b_c)
    y_diag = jnp.einsum('bcgjls,bcsgjp->bclgjp', cb[:, :, :, None] * decay, xdt)

    decay_to_end = jnp.exp(a_cs[..., -1:] - a_cs).transpose(0, 1, 4, 2, 3)
    states = jnp.einsum('bclgn,bclgjp->bcgjpn', b_c, xdt * decay_to_end[..., None])
    chunk_decay = jnp.exp(a_cs[..., -1])

    def step(carry, inp):
        st, dec = inp
        return carry * dec[..., None, None] + st, carry

    init = jnp.zeros((bsz, SSD_GROUPS, SSD_HPG, SSD_HEAD_DIM, SSD_STATE), f32)
    _, prev = lax.scan(step, init, (jnp.moveaxis(states, 1, 0), jnp.moveaxis(chunk_decay, 1, 0)))
    prev = jnp.moveaxis(prev, 0, 1)
    decay_from_start = jnp.exp(a_cs).transpose(0, 1, 4, 2, 3)
    y_off = jnp.einsum('bclgn,bcgjpn->bclgjp', c_c, prev) * decay_from_start[..., None]

    y = (y_diag + y_off).reshape(bsz, n_chunks * SSD_CHUNK, SSD_D_INNER)[:, pad:]
    y = y + xs.astype(f32) * jnp.repeat(d_skip.astype(f32), SSD_HEAD_DIM)
    g = (y * jax.nn.silu(z.astype(f32))).reshape(bsz, L, SSD_GROUPS, SSD_D_INNER // SSD_GROUPS)
    g = g * lax.rsqrt(jnp.mean(g * g, axis=-1, keepdims=True) + EPS)
    g = g.reshape(bsz, L, SSD_D_INNER) * norm_g.astype(f32)
    return g.astype(h.dtype) @ w_out


def rope_tables(L):
    inv = 1.0 / (ROPE_THETA ** (jnp.arange(0, MLA_ROPE, 2, dtype=jnp.float32) / MLA_ROPE))
    ang = jnp.arange(L, dtype=jnp.float32)[:, None] * inv[None, :]
    return jnp.cos(ang)[None, :, None, :], jnp.sin(ang)[None, :, None, :]


def apply_rope(t, cos, sin):
    t1, t2 = jnp.split(t, 2, axis=-1)
    cos = cos.astype(t.dtype)
    sin = sin.astype(t.dtype)
    return jnp.concatenate([t1 * cos - t2 * sin, t1 * sin + t2 * cos], axis=-1)


def mla_mixer(h, w_in, q_a_g, w_q_b, kv_a_g, w_kv_b, q_norm_g, k_norm_g, w_out):
    bsz, L, _ = h.shape
    q_lat, kv_lat, k_pe = jnp.split(h @ w_in, [MLA_Q_RANK, MLA_Q_RANK + MLA_KV_RANK], axis=-1)
    q = (rms_norm(q_lat, q_a_g) @ w_q_b).reshape(bsz, L, MLA_HEADS, MLA_QK)
    kv = (rms_norm(kv_lat, kv_a_g) @ w_kv_b).reshape(bsz, L, MLA_HEADS, MLA_NOPE + MLA_V)
    k_nope, v = jnp.split(kv, [MLA_NOPE], axis=-1)
    k = jnp.concatenate(
        [k_nope, jnp.broadcast_to(k_pe[:, :, None, :], (bsz, L, MLA_HEADS, MLA_ROPE))], axis=-1)
    q = rms_norm(q, q_norm_g)
    k = rms_norm(k, k_norm_g)
    cos, sin = rope_tables(L)
    q = jnp.concatenate([q[..., :MLA_NOPE], apply_rope(q[..., MLA_NOPE:], cos, sin)], axis=-1)
    k = jnp.concatenate([k[..., :MLA_NOPE], apply_rope(k[..., MLA_NOPE:], cos, sin)], axis=-1)
    scale = MLA_QK ** -0.5

    blocks = [(0, N_META)] + [(s, min(s + ATTN_BLOCK, L)) for s in range(N_META, L, ATTN_BLOCK)]
    outs = []
    for s, e in blocks:
        sc = jnp.einsum('bqhd,bkhd->bhqk', q[:, s:e], k[:, :e]).astype(jnp.float32) * scale
        mask = jnp.arange(e)[None, :] <= jnp.arange(s, e)[:, None]
        p = jax.nn.softmax(jnp.where(mask, sc, -jnp.inf), axis=-1).astype(v.dtype)
        outs.append(jnp.einsum('bhqk,bkhd->bqhd', p, v[:, :e]))
    o = jnp.concatenate(outs, axis=1).reshape(bsz, L, MLA_HEADS * MLA_V)
    return o @ w_out


def sqrelu_mlp(h, w_up, w_down):
    return jnp.square(jax.nn.relu(h @ w_up)) @ w_down


def setup_inputs(seed: int = 0) -> dict:
    key = jax.random.key(seed)
    ks = jax.random.split(key, 24)
    f32 = jnp.float32

    def nrm(k, shape, fan_in):
        return jax.random.normal(k, shape, f32) * (fan_in ** -0.5)

    def gain(k, shape):
        return 1.0 + 0.02 * jax.random.normal(k, shape, f32)

    ns, nm = N_SSD_LAYERS, N_MLA_LAYERS
    dt0 = jnp.exp(jax.random.uniform(ks[7], (ns, SSD_HEADS), f32, np.log(1e-3), np.log(1e-1)))
    return {
        "x": jax.random.normal(ks[0], (BATCH, SEQ, D_MODEL), f32),
        "meta_tokens": jax.random.normal(ks[1], (N_META, D_MODEL), f32),
        "ln_mix": gain(ks[2], (DEPTH, D_MODEL)),
        "ln_mlp": gain(ks[3], (DEPTH, D_MODEL)),
        "ssd_w_in": nrm(ks[4], (ns, D_MODEL, SSD_IN_DIM), D_MODEL),
        "ssd_conv_w": nrm(ks[5], (ns, SSD_CONV, SSD_CONV_DIM), SSD_CONV),
        "ssd_conv_b": 0.02 * jax.random.normal(ks[6], (ns, SSD_CONV_DIM), f32),
        "ssd_dt_bias": dt0 + jnp.log(-jnp.expm1(-dt0)),
        "ssd_a_log": jnp.log(jax.random.uniform(ks[8], (ns, SSD_HEADS), f32, 1.0, 16.0)),
        "ssd_d": 1.0 + 0.1 * jax.random.normal(ks[9], (ns, SSD_HEADS), f32),
        "ssd_norm": gain(ks[10], (ns, SSD_D_INNER)),
        "ssd_w_out": nrm(ks[11], (ns, SSD_D_INNER, D_MODEL), SSD_D_INNER),
        "mla_w_in": nrm(ks[12], (nm, D_MODEL, MLA_IN_DIM), D_MODEL),
        "mla_q_a_norm": gain(ks[13], (nm, MLA_Q_RANK)),
        "mla_w_q_b": nrm(ks[14], (nm, MLA_Q_RANK, MLA_HEADS * MLA_QK), MLA_Q_RANK),
        "mla_kv_a_norm": gain(ks[15], (nm, MLA_KV_RANK)),
        "mla_w_kv_b": nrm(ks[16], (nm, MLA_KV_RANK, MLA_HEADS * (MLA_NOPE + MLA_V)), MLA_KV_RANK),
        "mla_q_norm": gain(ks[17], (nm, MLA_QK)),
        "mla_k_norm": gain(ks[18], (nm, MLA_QK)),
        "mla_w_out": nrm(ks[19], (nm, MLA_HEADS * MLA_V, D_MODEL), MLA_HEADS * MLA_V),
        "mlp_w_up": nrm(ks[20], (DEPTH, D_MODEL, D_FF), D_MODEL),
        "mlp_w_down": nrm(ks[21], (DEPTH, D_FF, D_MODEL), D_FF),
    }


def reference(x, meta_tokens, ln_mix, ln_mlp, ssd_w_in, ssd_conv_w, ssd_conv_b, ssd_dt_bias,
              ssd_a_log, ssd_d, ssd_norm, ssd_w_out, mla_w_in, mla_q_a_norm, mla_w_q_b,
              mla_kv_a_norm, mla_w_kv_b, mla_q_norm, mla_k_norm, mla_w_out, mlp_w_up, mlp_w_down):
    bsz = x.shape[0]
    meta = jnp.broadcast_to(meta_tokens[None].astype(x.dtype), (bsz, N_META, D_MODEL))
    h = jnp.concatenate([meta, x], axis=1)
    for i in range(DEPTH):
        j = i // 2
        hn = rms_norm(h, ln_mix[i])
        if i % 2 == 0:
            h = h + ssd_mixer(hn, ssd_w_in[j], ssd_conv_w[j], ssd_conv_b[j], ssd_dt_bias[j],
                              ssd_a_log[j], ssd_d[j], ssd_norm[j], ssd_w_out[j])
        else:
            h = h + mla_mixer(hn, mla_w_in[j], mla_q_a_norm[j], mla_w_q_b[j], mla_kv_a_norm[j],
                              mla_w_kv_b[j], mla_q_norm[j], mla_k_norm[j], mla_w_out[j])
        h = h + sqrelu_mlp(rms_norm(h, ln_mlp[i]), mlp_w_up[i], mlp_w_down[i])
    return h[:, N_META:]
```

```python
import functools

import jax
import jax.numpy as jnp
from jax import lax
from jax.experimental import pallas as pl
from jax.experimental.pallas import tpu as pltpu

F32 = jnp.float32
BF16 = jnp.bfloat16

EPS = 1e-6
ROPE_THETA = 10000.0
N_META = 16
SSD_HEAD_DIM = 64
SSD_STATE = 128
SSD_GROUPS = 8
SSD_CHUNK = 128
MLA_HEADS = 16
MLA_NOPE = 64
MLA_ROPE = 32
MLA_V = 64
MLA_Q_RANK = 384
MLA_KV_RANK = 256

LANES = 128
VMEM_LIMIT_BYTES = 56 * 1024 * 1024
NEG_BIG = -1e30


def _params(n_axes):
    return pltpu.CompilerParams(
        dimension_semantics=("arbitrary",) * n_axes,
        vmem_limit_bytes=VMEM_LIMIT_BYTES)


def _pick_tile(n, candidates):
    for c in candidates:
        if n % c == 0:
            return c
    raise ValueError(f"no tile in {candidates} divides {n}")


def _full_spec(a):
    nd = a.ndim
    return pl.BlockSpec(a.shape, lambda *_: (0,) * nd)


def _rms(x, gain):
    ms = jnp.mean(x * x, axis=-1, keepdims=True)
    return x * lax.rsqrt(ms + EPS) * gain


def _sigmoid(x):
    return 1.0 / (1.0 + jnp.exp(-x))


def _dot(a, b):
    return jnp.dot(a, b, preferred_element_type=F32)


def _dot_nt(a, b):
    return lax.dot_general(a, b, (((1,), (1,)), ((), ())), preferred_element_type=F32)


def _norm_matmul_kernel(h_ref, g_ref, *refs, n_out, col_chunk):
    w_refs, o_refs = refs[:n_out], refs[n_out:]
    hn = _rms(h_ref[...], g_ref[...]).astype(BF16)
    for w_ref, o_ref in zip(w_refs, o_refs):
        n = w_ref.shape[1]
        for c0 in range(0, n, col_chunk):
            c1 = min(c0 + col_chunk, n)
            o_ref[:, c0:c1] = _dot(hn, w_ref[:, c0:c1]).astype(o_ref.dtype)


def norm_matmul(h, gain, weights, out_dtypes, *, name):
    t, d = h.shape
    tm = _pick_tile(t, (256, 128))
    kern = functools.partial(_norm_matmul_kernel, n_out=len(weights), col_chunk=1024)
    return pl.pallas_call(
        kern,
        grid=(t // tm,),
        in_specs=[pl.BlockSpec((tm, d), lambda i: (i, 0)), _full_spec(gain)]
        + [_full_spec(w) for w in weights],
        out_specs=[pl.BlockSpec((tm, w.shape[1]), lambda i: (i, 0)) for w in weights],
        out_shape=[jax.ShapeDtypeStruct((t, w.shape[1]), dt) for w, dt in zip(weights, out_dtypes)],
        compiler_params=_params(1),
        name=name,
    )(h, gain, *weights)


def _matmul_residual_kernel(a_ref, w_ref, h_ref, o_ref):
    o_ref[...] = h_ref[...] + _dot(a_ref[...], w_ref[...])


def matmul_residual(a, w, h, *, name):
    t, k = a.shape
    d = w.shape[1]
    tm = _pick_tile(t, (512, 256, 128))
    return pl.pallas_call(
        _matmul_residual_kernel,
        grid=(t // tm,),
        in_specs=[pl.BlockSpec((tm, k), lambda i: (i, 0)), _full_spec(w),
                  pl.BlockSpec((tm, d), lambda i: (i, 0))],
        out_specs=pl.BlockSpec((tm, d), lambda i: (i, 0)),
        out_shape=jax.ShapeDtypeStruct((t, d), F32),
        input_output_aliases={2: 0},
        compiler_params=_params(1),
        name=name,
    )(a, w, h)


def _mlp_kernel(h_ref, g_ref, wu_ref, wd_ref, o_ref, *, ff_chunk):
    x = h_ref[...]
    hn = _rms(x, g_ref[...]).astype(BF16)
    acc = x
    for c0 in range(0, wu_ref.shape[1], ff_chunk):
        u = jnp.maximum(_dot(hn, wu_ref[:, c0:c0 + ff_chunk]), 0.0)
        acc = acc + _dot((u * u).astype(BF16), wd_ref[c0:c0 + ff_chunk, :])
    o_ref[...] = acc


def mlp_block(h, gain, w_up, w_down, *, name):
    t, d = h.shape
    tm = _pick_tile(t, (512, 256, 128))
    kern = functools.partial(_mlp_kernel, ff_chunk=1024)
    return pl.pallas_call(
        kern,
        grid=(t // tm,),
        in_specs=[pl.BlockSpec((tm, d), lambda i: (i, 0)), _full_spec(gain),
                  _full_spec(w_up), _full_spec(w_down)],
        out_specs=pl.BlockSpec((tm, d), lambda i: (i, 0)),
        out_shape=jax.ShapeDtypeStruct((t, d), F32),
        input_output_aliases={0: 0},
        compiler_params=_params(1),
        name=name,
    )(h, gain, w_up, w_down)


def _split3_bf16(x):
    p1 = x.astype(BF16)
    r1 = x - p1.astype(F32)
    p2 = r1.astype(BF16)
    p3 = (r1 - p2.astype(F32)).astype(BF16)
    return p1, p2, p3


def _ssd_kernel(z_ref, xbc_ref, dt_ref, cw_ref, cb_ref, dtb_ref, alog_ref, dsk_ref, ng_ref,
                o_ref, ext_ref, xc_ref, st_ref, acs_ref, acst_ref, dts_ref,
                *, n_groups, hpg, hdim, n_state, chunk, conv_k):
    c = pl.program_id(1)
    gw = hpg * hdim
    d_inner = n_groups * gw
    conv_dim = d_inner + 2 * n_groups * n_state
    tail = 8

    @pl.when(c == 0)
    def _():
        ext_ref[0:tail, :] = jnp.zeros((tail, conv_dim), F32)
        st_ref[...] = jnp.zeros_like(st_ref)

    @pl.when(c != 0)
    def _():
        ext_ref[0:tail, :] = ext_ref[chunk:chunk + tail, :]

    ext_ref[tail:tail + chunk, :] = xbc_ref[...].astype(F32)

    cc = 512
    for c0 in range(0, conv_dim, cc):
        acc = cb_ref[:, c0:c0 + cc] + cw_ref[conv_k - 1:conv_k, c0:c0 + cc] * ext_ref[tail:tail + chunk, c0:c0 + cc]
        for k in range(conv_k - 1):
            r0 = tail - (conv_k - 1) + k
            acc = acc + cw_ref[k:k + 1, c0:c0 + cc] * ext_ref[r0:r0 + chunk, c0:c0 + cc]
        xc_ref[:, c0:c0 + cc] = acc * _sigmoid(acc)

    dtr = dt_ref[...] + dtb_ref[...]
    dt = jnp.maximum(dtr, 0.0) + jnp.log(1.0 + jnp.exp(-jnp.abs(dtr)))
    dts_ref[...] = dt
    a_dt = dt * (-jnp.exp(alog_ref[...]))
    row_i = lax.broadcasted_iota(jnp.int32, (chunk, chunk), 0)
    col_i = lax.broadcasted_iota(jnp.int32, (chunk, chunk), 1)
    causal = row_i >= col_i
    tri = jnp.where(causal, 1.0, 0.0).astype(BF16)
    p1, p2, p3 = _split3_bf16(a_dt)
    a_cs = _dot(tri, p1) + _dot(tri, p2) + _dot(tri, p3)
    acs_ref[...] = a_cs
    acst_ref[...] = a_cs.T

    lane = lax.broadcasted_iota(jnp.int32, (chunk, LANES), 1)
    first_half = lane < hdim
    b_off = d_inner
    c_off = d_inner + n_groups * n_state

    for g in range(n_groups):
        xs_g = xc_ref[:, g * gw:(g + 1) * gw]
        b_f32 = xc_ref[:, b_off + g * n_state:b_off + (g + 1) * n_state]
        b_g = b_f32.astype(BF16)
        c_g = xc_ref[:, c_off + g * n_state:c_off + (g + 1) * n_state].astype(BF16)
        cb = _dot_nt(c_g, b_g)
        prev_t = st_ref[g]
        y_off = _dot(c_g, prev_t.astype(BF16))
        y_parts, acs_parts, dt_parts = [], [], []
        for pr in range(hpg // 2):
            xs_p = xs_g[:, pr * LANES:(pr + 1) * LANES]
            heads = (g * hpg + 2 * pr, g * hpg + 2 * pr + 1)
            acc = None
            for jj, hh in enumerate(heads):
                col = acs_ref[:, hh:hh + 1]
                rowv = acst_ref[hh:hh + 1, :]
                decay = jnp.exp(jnp.where(causal, col - rowv, -jnp.inf))
                m = (cb * decay).astype(BF16)
                half = first_half if jj == 0 else jnp.logical_not(first_half)
                r = (xs_p * jnp.where(half, dts_ref[:, hh:hh + 1], 0.0)).astype(BF16)
                d = _dot(m, r)
                acc = d if acc is None else acc + d
            y_parts.append(acc)
            acs_parts.append(jnp.where(first_half, acs_ref[:, heads[0]:heads[0] + 1],
                                       acs_ref[:, heads[1]:heads[1] + 1]))
            dt_parts.append(jnp.where(first_half, dts_ref[:, heads[0]:heads[0] + 1],
                                      dts_ref[:, heads[1]:heads[1] + 1]))
        y_diag = jnp.concatenate(y_parts, axis=1)
        acs_x = jnp.concatenate(acs_parts, axis=1)
        dt_x = jnp.concatenate(dt_parts, axis=1)
        a_last = acs_x[chunk - 1:chunk, :]
        y = y_diag + y_off * jnp.exp(acs_x) + xs_g * dsk_ref[:, g * gw:(g + 1) * gw]
        xdtd = (xs_g * (dt_x * jnp.exp(a_last - acs_x))).astype(BF16)
        st_ref[g] = prev_t * jnp.exp(a_last) + _dot(b_f32.T.astype(BF16), xdtd)
        zg = z_ref[:, g * gw:(g + 1) * gw].astype(F32)
        gated = y * (zg * _sigmoid(zg))
        o_ref[:, g * gw:(g + 1) * gw] = _rms(gated, ng_ref[:, g * gw:(g + 1) * gw]).astype(o_ref.dtype)


def ssd_core(z, xbc, dt_raw, conv_w, conv_b, dt_bias, a_log, d_skip, norm_g, *, batch, name):
    t, d_inner = z.shape
    conv_dim = xbc.shape[1]
    n_state = SSD_STATE
    n_groups = (conv_dim - d_inner) // (2 * n_state)
    hdim = SSD_HEAD_DIM
    hpg = d_inner // (n_groups * hdim)
    assert 2 * hdim == LANES and hpg % 2 == 0
    chunk = SSD_CHUNK
    nc = t // (batch * chunk)
    conv_k = conv_w.shape[0]
    kern = functools.partial(_ssd_kernel, n_groups=n_groups, hpg=hpg, hdim=hdim,
                             n_state=n_state, chunk=chunk, conv_k=conv_k)
    row = lambda b, c: (b * nc + c, 0)
    return pl.pallas_call(
        kern,
        grid=(batch, nc),
        in_specs=[pl.BlockSpec((chunk, d_inner), row),
                  pl.BlockSpec((chunk, conv_dim), row),
                  pl.BlockSpec((chunk, LANES), row),
                  _full_spec(conv_w), _full_spec(conv_b), _full_spec(dt_bias),
                  _full_spec(a_log), _full_spec(d_skip), _full_spec(norm_g)],
        out_specs=pl.BlockSpec((chunk, d_inner), row),
        out_shape=jax.ShapeDtypeStruct((t, d_inner), BF16),
        scratch_shapes=[
            pltpu.VMEM((chunk + 8, conv_dim), F32),
            pltpu.VMEM((chunk, conv_dim), F32),
            pltpu.VMEM((n_groups, n_state, hpg * hdim), F32),
            pltpu.VMEM((chunk, LANES), F32),
            pltpu.VMEM((LANES, chunk), F32),
            pltpu.VMEM((chunk, LANES), F32),
        ],
        compiler_params=_params(2),
        name=name,
    )(z, xbc, dt_raw, conv_w, conv_b, dt_bias, a_log, d_skip, norm_g)


def _pad_cols(a, n):
    return jnp.pad(a, [(0, 0)] * (a.ndim - 1) + [(0, n - a.shape[-1])])


def ssd_layer(h, ln_g, w_in, conv_w, conv_b, dt_bias, a_log, d_skip, norm_g, w_out, *, batch, idx):
    d_inner = w_out.shape[0]
    n_heads = dt_bias.shape[0]
    conv_dim = conv_w.shape[1]
    w_z = w_in[:, :d_inner].astype(BF16)
    w_xbc = w_in[:, d_inner:d_inner + conv_dim].astype(BF16)
    w_dt = _pad_cols(w_in[:, d_inner + conv_dim:], LANES).astype(BF16)
    z, xbc, dt_raw = norm_matmul(h, ln_g[None], [w_z, w_xbc, w_dt], [BF16, BF16, F32],
                                 name=f"ssd_in_proj_{idx}")
    gated = ssd_core(z, xbc, dt_raw, conv_w, conv_b[None], _pad_cols(dt_bias[None], LANES),
                     _pad_cols(a_log[None], LANES), jnp.repeat(d_skip, d_inner // n_heads)[None],
                     norm_g[None], batch=batch, name=f"ssd_core_{idx}")
    return matmul_residual(gated, w_out.astype(BF16), h, name=f"ssd_out_proj_{idx}")


def _mla_prep_kernel(h_ref, g_ref, win_ref, qag_ref, wq_ref, kvag_ref, wk_ref, wv_ref,
                     qg_ref, kg_ref, kpeg_ref, fq_ref, fk_ref, q_out, k_out, v_out,
                     *, n_heads, q_rank, kv_rank, qk_dim, rope_dim):
    hn = _rms(h_ref[...], g_ref[...]).astype(BF16)
    lat = _dot(hn, win_ref[...])
    qn = _rms(lat[:, :q_rank], qag_ref[...]).astype(BF16)
    kvn = _rms(lat[:, q_rank:q_rank + kv_rank], kvag_ref[...]).astype(BF16)
    kpe = lat[:, q_rank + kv_rank:]
    tm = kpe.shape[0]
    lane = lax.broadcasted_iota(jnp.int32, (tm, LANES), 1)

    v_out[...] = _dot(kvn, wv_ref[...]).astype(v_out.dtype)

    ss_pe = jnp.sum(jnp.where(lane < rope_dim, kpe * kpe, 0.0), axis=-1, keepdims=True)
    rot = kpe * kpeg_ref[...] * fk_ref[...]
    rot = rot + pltpu.roll(rot, rope_dim, 1)
    kf = jnp.where(lane >= LANES - 2 * rope_dim, rot, 0.0)

    k_raw = _dot(kvn, wk_ref[...])
    kg = kg_ref[...]
    for hh in range(n_heads):
        kh = k_raw[:, hh * LANES:(hh + 1) * LANES]
        ss = jnp.sum(kh * kh, axis=-1, keepdims=True) + ss_pe
        r = lax.rsqrt(ss * (1.0 / qk_dim) + EPS)
        k_out[:, hh * LANES:(hh + 1) * LANES] = ((kh * kg + kf) * r).astype(k_out.dtype)

    q_raw = _dot(qn, wq_ref[...])
    gf = qg_ref[...] * fq_ref[...]
    for hh in range(n_heads):
        qh = q_raw[:, hh * LANES:(hh + 1) * LANES]
        ss = jnp.sum(jnp.where(lane < qk_dim, qh * qh, 0.0), axis=-1, keepdims=True)
        r = lax.rsqrt(ss * (1.0 / qk_dim) + EPS)
        q_out[:, hh * LANES:(hh + 1) * LANES] = (qh * gf * r).astype(q_out.dtype)


def _swap_halves(a):
    n = a.shape[-1] // 2
    return jnp.concatenate([a[..., n:], a[..., :n]], axis=-1)


def mla_prep(h, ln_g, w_in, q_a_g, w_q_b, kv_a_g, w_kv_b, q_norm_g, k_norm_g, *, seq_pad, name):
    t, d = h.shape
    nh, nope, rope, vd = MLA_HEADS, MLA_NOPE, MLA_ROPE, MLA_V
    qk = nope + rope
    q_rank, kv_rank = q_a_g.shape[0], kv_a_g.shape[0]
    assert nope + 2 * rope == LANES
    tm = _pick_tile(seq_pad, (384, 256, 128))

    kpe_w = w_in[:, q_rank + kv_rank:]
    kpe_blk = jnp.concatenate([kpe_w, _swap_halves(kpe_w)] * 2, axis=1)
    w_in_l = jnp.concatenate([w_in[:, :q_rank + kv_rank], kpe_blk], axis=1).astype(BF16)

    wq = w_q_b.reshape(q_rank, nh, qk)
    wq_l = jnp.concatenate([wq, _swap_halves(wq[..., nope:])], axis=-1).reshape(q_rank, nh * LANES).astype(BF16)
    wkv = w_kv_b.reshape(kv_rank, nh, nope + vd)
    wk_l = _pad_cols(wkv[..., :nope], LANES).reshape(kv_rank, nh * LANES).astype(BF16)
    wv_l = wkv[..., nope:].reshape(kv_rank, nh * vd).astype(BF16)

    scale = qk ** -0.5
    qg_l = (jnp.concatenate([q_norm_g, _swap_halves(q_norm_g[nope:])]) * scale)[None]
    kg_l = _pad_cols(k_norm_g[:nope], LANES)[None]
    kpeg_l = jnp.concatenate([k_norm_g[nope:], _swap_halves(k_norm_g[nope:])] * 2)[None]

    inv = 1.0 / (ROPE_THETA ** (jnp.arange(0, rope, 2, dtype=F32) / rope))
    ang = jnp.arange(seq_pad, dtype=F32)[:, None] * inv[None, :]
    cos, sin = jnp.cos(ang), jnp.sin(ang)
    rot_f = jnp.concatenate([cos, cos, -sin, sin], axis=1)
    fq = jnp.concatenate([jnp.ones((seq_pad, nope), F32), rot_f], axis=1)
    fk = jnp.concatenate([rot_f, rot_f], axis=1)

    n_pos = seq_pad // tm
    kern = functools.partial(_mla_prep_kernel, n_heads=nh, q_rank=q_rank, kv_rank=kv_rank,
                             qk_dim=qk, rope_dim=rope)
    row = lambda i: (i, 0)
    pos = lambda i: (i % n_pos, 0)
    params = [ln_g[None], w_in_l, q_a_g[None], wq_l, kv_a_g[None], wk_l, wv_l, qg_l, kg_l, kpeg_l]
    return pl.pallas_call(
        kern,
        grid=(t // tm,),
        in_specs=[pl.BlockSpec((tm, d), row)] + [_full_spec(p) for p in params]
        + [pl.BlockSpec((tm, LANES), pos), pl.BlockSpec((tm, LANES), pos)],
        out_specs=[pl.BlockSpec((tm, nh * LANES), row), pl.BlockSpec((tm, nh * LANES), row),
                   pl.BlockSpec((tm, nh * vd), row)],
        out_shape=[jax.ShapeDtypeStruct((t, nh * LANES), BF16),
                   jax.ShapeDtypeStruct((t, nh * LANES), BF16),
                   jax.ShapeDtypeStruct((t, nh * vd), BF16)],
        compiler_params=_params(1),
        name=name,
    )(h, *params, fq, fk)


def _attn_kernel(q_ref, k_ref, v_ref, o_ref, m_ref, l_ref, acc_ref, *, tq, vd):
    qi = pl.program_id(2)
    m_ref[...] = jnp.full(m_ref.shape, NEG_BIG, F32)
    l_ref[...] = jnp.zeros(l_ref.shape, F32)
    acc_ref[...] = jnp.zeros(acc_ref.shape, F32)
    lane = lax.broadcasted_iota(jnp.int32, (tq, LANES), 1)
    first = lane < vd
    rep = tq // LANES

    def block(kb, masked):
        r0 = pl.multiple_of(kb * tq, tq)
        v = v_ref[pl.ds(r0, tq), :]
        pv = []
        alphas = []
        for j in range(2):
            q = q_ref[:, j * LANES:(j + 1) * LANES]
            k = k_ref[pl.ds(r0, tq), j * LANES:(j + 1) * LANES]
            s = _dot_nt(q, k)
            if masked:
                ri = lax.broadcasted_iota(jnp.int32, (tq, tq), 0)
                ci = lax.broadcasted_iota(jnp.int32, (tq, tq), 1)
                s = jnp.where(ri >= ci, s, NEG_BIG)
            m_prev = m_ref[j]
            m_new = jnp.maximum(m_prev, jnp.max(s, axis=-1, keepdims=True))
            alpha = jnp.exp(m_prev - m_new)
            p = jnp.exp(s - jnp.concatenate([m_new] * rep, axis=1))
            l_ref[j] = alpha * l_ref[j] + jnp.sum(p, axis=-1, keepdims=True)
            m_ref[j] = m_new
            vj = jnp.where(first if j == 0 else jnp.logical_not(first), v, jnp.zeros_like(v))
            pv.append(_dot(p.astype(BF16), vj))
            alphas.append(alpha)
        acc_ref[...] = acc_ref[...] * jnp.where(first, alphas[0], alphas[1]) + pv[0] + pv[1]

    def body(kb, carry):
        block(kb, False)
        return carry

    lax.fori_loop(0, qi, body, 0)
    block(qi, True)
    o_ref[...] = (acc_ref[...] / jnp.where(first, l_ref[0], l_ref[1])).astype(o_ref.dtype)


def attention(q, k, v, *, batch, seq_pad, name):
    t = q.shape[0]
    vd = MLA_V
    n_pairs = q.shape[1] // (2 * LANES)
    tq = _pick_tile(seq_pad, (384, 256, 128))
    nq = seq_pad // tq
    kern = functools.partial(_attn_kernel, tq=tq, vd=vd)
    return pl.pallas_call(
        kern,
        grid=(batch, n_pairs, nq),
        in_specs=[pl.BlockSpec((tq, 2 * LANES), lambda b, p, i: (b * nq + i, p)),
                  pl.BlockSpec((seq_pad, 2 * LANES), lambda b, p, i: (b, p)),
                  pl.BlockSpec((seq_pad, 2 * vd), lambda b, p, i: (b, p))],
        out_specs=pl.BlockSpec((tq, 2 * vd), lambda b, p, i: (b * nq + i, p)),
        out_shape=jax.ShapeDtypeStruct((t, n_pairs * 2 * vd), BF16),
        scratch_shapes=[pltpu.VMEM((2, tq, LANES), F32), pltpu.VMEM((2, tq, LANES), F32),
                        pltpu.VMEM((tq, LANES), F32)],
        compiler_params=_params(3),
        name=name,
    )(q, k, v)


def mla_layer(h, ln_g, w_in, q_a_g, w_q_b, kv_a_g, w_kv_b, q_norm_g, k_norm_g, w_out,
              *, batch, seq_pad, idx):
    q, k, v = mla_prep(h, ln_g, w_in, q_a_g, w_q_b, kv_a_g, w_kv_b, q_norm_g, k_norm_g,
                       seq_pad=seq_pad, name=f"mla_prep_{idx}")
    o = attention(q, k, v, batch=batch, seq_pad=seq_pad, name=f"mla_attn_{idx}")
    return matmul_residual(o, w_out.astype(BF16), h, name=f"mla_out_proj_{idx}")


def kernel(x, meta_tokens, ln_mix, ln_mlp, ssd_w_in, ssd_conv_w, ssd_conv_b, ssd_dt_bias, ssd_a_log, ssd_d, ssd_norm, ssd_w_out, mla_w_in, mla_q_a_norm, mla_w_q_b, mla_kv_a_norm, mla_w_kv_b, mla_q_norm, mla_k_norm, mla_w_out, mlp_w_up, mlp_w_down):
    bsz, seq, d = x.shape
    n_meta = meta_tokens.shape[0]
    depth = ln_mix.shape[0]
    seq_all = n_meta + seq
    seq_pad = -(-seq_all // SSD_CHUNK) * SSD_CHUNK
    meta = jnp.broadcast_to(meta_tokens[None].astype(x.dtype), (bsz, n_meta, d))
    h = jnp.concatenate([meta, x, jnp.zeros((bsz, seq_pad - seq_all, d), x.dtype)], axis=1)
    h = h.reshape(bsz * seq_pad, d)
    for i in range(depth):
        j = i // 2
        if i % 2 == 0:
            h = ssd_layer(h, ln_mix[i], ssd_w_in[j], ssd_conv_w[j], ssd_conv_b[j], ssd_dt_bias[j],
                          ssd_a_log[j], ssd_d[j], ssd_norm[j], ssd_w_out[j], batch=bsz, idx=j)
        else:
            h = mla_layer(h, ln_mix[i], mla_w_in[j], mla_q_a_norm[j], mla_w_q_b[j], mla_kv_a_norm[j],
                          mla_w_kv_b[j], mla_q_norm[j], mla_k_norm[j], mla_w_out[j],
                          batch=bsz, seq_pad=seq_pad, idx=j)
        h = mlp_block(h, ln_mlp[i][None], mlp_w_up[i].astype(BF16), mlp_w_down[i].astype(BF16),
                      name=f"mlp_{i}")
    return h.reshape(bsz, seq_pad, d)[:, n_meta:seq_all]
```

```python
import functools

import jax
import jax.numpy as jnp
from jax import lax
from jax.experimental import pallas as pl
from jax.experimental.pallas import tpu as pltpu

F32 = jnp.float32
BF16 = jnp.bfloat16

EPS = 1e-6
ROPE_THETA = 10000.0
N_META = 16
SSD_HEAD_DIM = 64
SSD_STATE = 128
SSD_GROUPS = 8
SSD_CHUNK = 128
MLA_HEADS = 16
MLA_NOPE = 64
MLA_ROPE = 32
MLA_V = 64
MLA_Q_RANK = 384
MLA_KV_RANK = 256

LANES = 128
VMEM_LIMIT_BYTES = 56 * 1024 * 1024
NEG_BIG = -1e30
LOG2_E = 1.4426950408889634


def _params(n_axes):
    return pltpu.CompilerParams(
        dimension_semantics=("arbitrary",) * n_axes,
        vmem_limit_bytes=VMEM_LIMIT_BYTES)


def _pick_tile(n, candidates):
    for c in candidates:
        if n % c == 0:
            return c
    raise ValueError(f"no tile in {candidates} divides {n}")


def _full_spec(a):
    nd = a.ndim
    return pl.BlockSpec(a.shape, lambda *_: (0,) * nd)


def _rms(x, gain):
    ms = jnp.mean(x * x, axis=-1, keepdims=True)
    return x * lax.rsqrt(ms + EPS) * gain


def _sigmoid(x):
    return 1.0 / (1.0 + jnp.exp(-x))


def _dot(a, b):
    return jnp.dot(a, b, preferred_element_type=F32)


def _dot_nt(a, b):
    return lax.dot_general(a, b, (((1,), (1,)), ((), ())), preferred_element_type=F32)


def _norm_matmul_kernel(h_ref, g_ref, *refs, n_out, col_chunk):
    w_refs, o_refs = refs[:n_out], refs[n_out:]
    hn = _rms(h_ref[...], g_ref[...]).astype(BF16)
    for w_ref, o_ref in zip(w_refs, o_refs):
        n = w_ref.shape[1]
        for c0 in range(0, n, col_chunk):
            c1 = min(c0 + col_chunk, n)
            o_ref[:, c0:c1] = _dot(hn, w_ref[:, c0:c1]).astype(o_ref.dtype)


def norm_matmul(h, gain, weights, out_dtypes, *, name):
    t, d = h.shape
    tm = _pick_tile(t, (256, 128))
    kern = functools.partial(_norm_matmul_kernel, n_out=len(weights), col_chunk=1024)
    return pl.pallas_call(
        kern,
        grid=(t // tm,),
        in_specs=[pl.BlockSpec((tm, d), lambda i: (i, 0)), _full_spec(gain)]
        + [_full_spec(w) for w in weights],
        out_specs=[pl.BlockSpec((tm, w.shape[1]), lambda i: (i, 0)) for w in weights],
        out_shape=[jax.ShapeDtypeStruct((t, w.shape[1]), dt) for w, dt in zip(weights, out_dtypes)],
        compiler_params=_params(1),
        name=name,
    )(h, gain, *weights)


def _matmul_residual_kernel(a_ref, w_ref, h_ref, o_ref):
    o_ref[...] = h_ref[...] + _dot(a_ref[...], w_ref[...])


def matmul_residual(a, w, h, *, name):
    t, k = a.shape
    d = w.shape[1]
    tm = _pick_tile(t, (512, 256, 128))
    return pl.pallas_call(
        _matmul_residual_kernel,
        grid=(t // tm,),
        in_specs=[pl.BlockSpec((tm, k), lambda i: (i, 0)), _full_spec(w),
                  pl.BlockSpec((tm, d), lambda i: (i, 0))],
        out_specs=pl.BlockSpec((tm, d), lambda i: (i, 0)),
        out_shape=jax.ShapeDtypeStruct((t, d), F32),
        input_output_aliases={2: 0},
        compiler_params=_params(1),
        name=name,
    )(a, w, h)


def _mlp_kernel(h_ref, g_ref, wu_ref, wd_ref, o_ref, *, ff_chunk):
    x = h_ref[...]
    hn = _rms(x, g_ref[...]).astype(BF16)
    acc = x
    for c0 in range(0, wu_ref.shape[1], ff_chunk):
        u = jnp.maximum(_dot(hn, wu_ref[:, c0:c0 + ff_chunk]), 0.0)
        acc = acc + _dot((u * u).astype(BF16), wd_ref[c0:c0 + ff_chunk, :])
    o_ref[...] = acc


def mlp_block(h, gain, w_up, w_down, *, name):
    t, d = h.shape
    tm = _pick_tile(t, (512, 256, 128))
    kern = functools.partial(_mlp_kernel, ff_chunk=1024)
    return pl.pallas_call(
        kern,
        grid=(t // tm,),
        in_specs=[pl.BlockSpec((tm, d), lambda i: (i, 0)), _full_spec(gain),
                  _full_spec(w_up), _full_spec(w_down)],
        out_specs=pl.BlockSpec((tm, d), lambda i: (i, 0)),
        out_shape=jax.ShapeDtypeStruct((t, d), F32),
        input_output_aliases={0: 0},
        compiler_params=_params(1),
        name=name,
    )(h, gain, w_up, w_down)


def _split3_bf16(x):
    p1 = x.astype(BF16)
    r1 = x - p1.astype(F32)
    p2 = r1.astype(BF16)
    p3 = (r1 - p2.astype(F32)).astype(BF16)
    return p1, p2, p3


def _ssd_kernel(z_ref, xbc_ref, dt_ref, cw_ref, cb_ref, dtb_ref, alog_ref, dsk_ref, ng_ref,
                o_ref, ext_ref, xc_ref, st_ref, acs_ref, acst_ref, dts_ref,
                *, n_groups, hpg, hdim, n_state, chunk, conv_k):
    c = pl.program_id(1)
    gw = hpg * hdim
    d_inner = n_groups * gw
    conv_dim = d_inner + 2 * n_groups * n_state
    tail = 8

    @pl.when(c == 0)
    def _():
        ext_ref[0:tail, :] = jnp.zeros((tail, conv_dim), F32)
        st_ref[...] = jnp.zeros_like(st_ref)

    @pl.when(c != 0)
    def _():
        ext_ref[0:tail, :] = ext_ref[chunk:chunk + tail, :]

    ext_ref[tail:tail + chunk, :] = xbc_ref[...].astype(F32)

    cc = 512
    for c0 in range(0, conv_dim, cc):
        acc = cb_ref[:, c0:c0 + cc] + cw_ref[conv_k - 1:conv_k, c0:c0 + cc] * ext_ref[tail:tail + chunk, c0:c0 + cc]
        for k in range(conv_k - 1):
            r0 = tail - (conv_k - 1) + k
            acc = acc + cw_ref[k:k + 1, c0:c0 + cc] * ext_ref[r0:r0 + chunk, c0:c0 + cc]
        xc_ref[:, c0:c0 + cc] = acc * _sigmoid(acc)

    dtr = dt_ref[...] + dtb_ref[...]
    dt = jnp.maximum(dtr, 0.0) + jnp.log(1.0 + jnp.exp(-jnp.abs(dtr)))
    dts_ref[...] = dt
    a_dt = dt * (-jnp.exp(alog_ref[...]))
    row_i = lax.broadcasted_iota(jnp.int32, (chunk, chunk), 0)
    col_i = lax.broadcasted_iota(jnp.int32, (chunk, chunk), 1)
    causal = row_i >= col_i
    tri = jnp.where(causal, 1.0, 0.0).astype(BF16)
    p1, p2, p3 = _split3_bf16(a_dt)
    a_cs = _dot(tri, p1) + _dot(tri, p2) + _dot(tri, p3)
    acs_ref[...] = a_cs
    acst_ref[...] = a_cs.T

    lane = lax.broadcasted_iota(jnp.int32, (chunk, LANES), 1)
    first_half = lane < hdim
    b_off = d_inner
    c_off = d_inner + n_groups * n_state

    for g in range(n_groups):
        xs_g = xc_ref[:, g * gw:(g + 1) * gw]
        b_f32 = xc_ref[:, b_off + g * n_state:b_off + (g + 1) * n_state]
        b_g = b_f32.astype(BF16)
        c_g = xc_ref[:, c_off + g * n_state:c_off + (g + 1) * n_state].astype(BF16)
        cb = _dot_nt(c_g, b_g)
        prev_t = st_ref[g]
        y_off = _dot(c_g, prev_t.astype(BF16))
        y_parts, acs_parts, dt_parts = [], [], []
        for pr in range(hpg // 2):
            xs_p = xs_g[:, pr * LANES:(pr + 1) * LANES]
            heads = (g * hpg + 2 * pr, g * hpg + 2 * pr + 1)
            acc = None
            for jj, hh in enumerate(heads):
                col = acs_ref[:, hh:hh + 1]
                rowv = acst_ref[hh:hh + 1, :]
                decay = jnp.exp(jnp.where(causal, col - rowv, -jnp.inf))
                m = (cb * decay).astype(BF16)
                half = first_half if jj == 0 else jnp.logical_not(first_half)
                r = (xs_p * jnp.where(half, dts_ref[:, hh:hh + 1], 0.0)).astype(BF16)
                d = _dot(m, r)
                acc = d if acc is None else acc + d
            y_parts.append(acc)
            acs_parts.append(jnp.where(first_half, acs_ref[:, heads[0]:heads[0] + 1],
                                       acs_ref[:, heads[1]:heads[1] + 1]))
            dt_parts.append(jnp.where(first_half, dts_ref[:, heads[0]:heads[0] + 1],
                                      dts_ref[:, heads[1]:heads[1] + 1]))
        y_diag = jnp.concatenate(y_parts, axis=1)
        acs_x = jnp.concatenate(acs_parts, axis=1)
        dt_x = jnp.concatenate(dt_parts, axis=1)
        a_last = acs_x[chunk - 1:chunk, :]
        y = y_diag + y_off * jnp.exp(acs_x) + xs_g * dsk_ref[:, g * gw:(g + 1) * gw]
        xdtd = (xs_g * (dt_x * jnp.exp(a_last - acs_x))).astype(BF16)
        st_ref[g] = prev_t * jnp.exp(a_last) + _dot(b_f32.T.astype(BF16), xdtd)
        zg = z_ref[:, g * gw:(g + 1) * gw].astype(F32)
        gated = y * (zg * _sigmoid(zg))
        o_ref[:, g * gw:(g + 1) * gw] = _rms(gated, ng_ref[:, g * gw:(g + 1) * gw]).astype(o_ref.dtype)


def ssd_core(z, xbc, dt_raw, conv_w, conv_b, dt_bias, a_log, d_skip, norm_g, *, batch, name):
    t, d_inner = z.shape
    conv_dim = xbc.shape[1]
    n_state = SSD_STATE
    n_groups = (conv_dim - d_inner) // (2 * n_state)
    hdim = SSD_HEAD_DIM
    hpg = d_inner // (n_groups * hdim)
    assert 2 * hdim == LANES and hpg % 2 == 0
    chunk = SSD_CHUNK
    nc = t // (batch * chunk)
    conv_k = conv_w.shape[0]
    kern = functools.partial(_ssd_kernel, n_groups=n_groups, hpg=hpg, hdim=hdim,
                             n_state=n_state, chunk=chunk, conv_k=conv_k)
    row = lambda b, c: (b * nc + c, 0)
    return pl.pallas_call(
        kern,
        grid=(batch, nc),
        in_specs=[pl.BlockSpec((chunk, d_inner), row),
                  pl.BlockSpec((chunk, conv_dim), row),
                  pl.BlockSpec((chunk, LANES), row),
                  _full_spec(conv_w), _full_spec(conv_b), _full_spec(dt_bias),
                  _full_spec(a_log), _full_spec(d_skip), _full_spec(norm_g)],
        out_specs=pl.BlockSpec((chunk, d_inner), row),
        out_shape=jax.ShapeDtypeStruct((t, d_inner), BF16),
        scratch_shapes=[
            pltpu.VMEM((chunk + 8, conv_dim), F32),
            pltpu.VMEM((chunk, conv_dim), F32),
            pltpu.VMEM((n_groups, n_state, hpg * hdim), F32),
            pltpu.VMEM((chunk, LANES), F32),
            pltpu.VMEM((LANES, chunk), F32),
            pltpu.VMEM((chunk, LANES), F32),
        ],
        compiler_params=_params(2),
        name=name,
    )(z, xbc, dt_raw, conv_w, conv_b, dt_bias, a_log, d_skip, norm_g)


def _pad_cols(a, n):
    return jnp.pad(a, [(0, 0)] * (a.ndim - 1) + [(0, n - a.shape[-1])])


def ssd_layer(h, ln_g, w_in, conv_w, conv_b, dt_bias, a_log, d_skip, norm_g, w_out, *, batch, idx):
    d_inner = w_out.shape[0]
    n_heads = dt_bias.shape[0]
    conv_dim = conv_w.shape[1]
    w_z = w_in[:, :d_inner].astype(BF16)
    w_xbc = w_in[:, d_inner:d_inner + conv_dim].astype(BF16)
    w_dt = _pad_cols(w_in[:, d_inner + conv_dim:], LANES).astype(BF16)
    z, xbc, dt_raw = norm_matmul(h, ln_g[None], [w_z, w_xbc, w_dt], [BF16, BF16, F32],
                                 name=f"ssd_in_proj_{idx}")
    gated = ssd_core(z, xbc, dt_raw, conv_w, conv_b[None], _pad_cols(dt_bias[None], LANES),
                     _pad_cols(a_log[None], LANES), jnp.repeat(d_skip, d_inner // n_heads)[None],
                     norm_g[None], batch=batch, name=f"ssd_core_{idx}")
    return matmul_residual(gated, w_out.astype(BF16), h, name=f"ssd_out_proj_{idx}")


def _mla_prep_kernel(h_ref, g_ref, win_ref, qag_ref, wq_ref, kvag_ref, wk_ref, wv_ref,
                     qg_ref, kg_ref, kpeg_ref, fq_ref, fk_ref, q_out, k_out, v_out,
                     *, n_heads, q_rank, kv_rank, qk_dim, rope_dim):
    hn = _rms(h_ref[...], g_ref[...]).astype(BF16)
    lat = _dot(hn, win_ref[...])
    qn = _rms(lat[:, :q_rank], qag_ref[...]).astype(BF16)
    kvn = _rms(lat[:, q_rank:q_rank + kv_rank], kvag_ref[...]).astype(BF16)
    kpe = lat[:, q_rank + kv_rank:]
    tm = kpe.shape[0]
    lane = lax.broadcasted_iota(jnp.int32, (tm, LANES), 1)

    v_out[...] = _dot_nt(wv_ref[...], kvn).astype(v_out.dtype)

    ss_pe = jnp.sum(jnp.where(lane < rope_dim, kpe * kpe, 0.0), axis=-1, keepdims=True)
    rot = kpe * kpeg_ref[...] * fk_ref[...]
    rot = rot + pltpu.roll(rot, rope_dim, 1)
    kf = jnp.where(lane >= LANES - 2 * rope_dim, rot, 0.0)

    k_raw = _dot(kvn, wk_ref[...])
    kg = kg_ref[...]
    for hh in range(n_heads):
        kh = k_raw[:, hh * LANES:(hh + 1) * LANES]
        ss = jnp.sum(kh * kh, axis=-1, keepdims=True) + ss_pe
        r = lax.rsqrt(ss * (1.0 / qk_dim) + EPS)
        k_out[:, hh * LANES:(hh + 1) * LANES] = ((kh * kg + kf) * r).astype(k_out.dtype)

    q_raw = _dot(qn, wq_ref[...])
    gf = qg_ref[...] * fq_ref[...]
    for hh in range(n_heads):
        qh = q_raw[:, hh * LANES:(hh + 1) * LANES]
        ss = jnp.sum(jnp.where(lane < qk_dim, qh * qh, 0.0), axis=-1, keepdims=True)
        r = lax.rsqrt(ss * (1.0 / qk_dim) + EPS)
        q_out[:, hh * LANES:(hh + 1) * LANES] = (qh * gf * r).astype(q_out.dtype)


def _swap_halves(a):
    n = a.shape[-1] // 2
    return jnp.concatenate([a[..., n:], a[..., :n]], axis=-1)


def mla_prep(h, ln_g, w_in, q_a_g, w_q_b, kv_a_g, w_kv_b, q_norm_g, k_norm_g, *, seq_pad, name):
    t, d = h.shape
    nh, nope, rope, vd = MLA_HEADS, MLA_NOPE, MLA_ROPE, MLA_V
    qk = nope + rope
    q_rank, kv_rank = q_a_g.shape[0], kv_a_g.shape[0]
    assert nope + 2 * rope == LANES
    tm = _pick_tile(seq_pad, (384, 256, 128))

    kpe_w = w_in[:, q_rank + kv_rank:]
    kpe_blk = jnp.concatenate([kpe_w, _swap_halves(kpe_w)] * 2, axis=1)
    w_in_l = jnp.concatenate([w_in[:, :q_rank + kv_rank], kpe_blk], axis=1).astype(BF16)

    wq = w_q_b.reshape(q_rank, nh, qk)
    wq_l = jnp.concatenate([wq, _swap_halves(wq[..., nope:])], axis=-1).reshape(q_rank, nh * LANES).astype(BF16)
    wkv = w_kv_b.reshape(kv_rank, nh, nope + vd)
    wk_l = _pad_cols(wkv[..., :nope], LANES).reshape(kv_rank, nh * LANES).astype(BF16)
    wv_l = wkv[..., nope:].reshape(kv_rank, nh * vd).T.astype(BF16)

    scale = qk ** -0.5 * LOG2_E
    qg_l = (jnp.concatenate([q_norm_g, _swap_halves(q_norm_g[nope:])]) * scale)[None]
    kg_l = _pad_cols(k_norm_g[:nope], LANES)[None]
    kpeg_l = jnp.concatenate([k_norm_g[nope:], _swap_halves(k_norm_g[nope:])] * 2)[None]

    inv = 1.0 / (ROPE_THETA ** (jnp.arange(0, rope, 2, dtype=F32) / rope))
    ang = jnp.arange(seq_pad, dtype=F32)[:, None] * inv[None, :]
    cos, sin = jnp.cos(ang), jnp.sin(ang)
    rot_f = jnp.concatenate([cos, cos, -sin, sin], axis=1)
    fq = jnp.concatenate([jnp.ones((seq_pad, nope), F32), rot_f], axis=1)
    fk = jnp.concatenate([rot_f, rot_f], axis=1)

    n_pos = seq_pad // tm
    kern = functools.partial(_mla_prep_kernel, n_heads=nh, q_rank=q_rank, kv_rank=kv_rank,
                             qk_dim=qk, rope_dim=rope)
    row = lambda i: (i, 0)
    pos = lambda i: (i % n_pos, 0)
    params = [ln_g[None], w_in_l, q_a_g[None], wq_l, kv_a_g[None], wk_l, wv_l, qg_l, kg_l, kpeg_l]
    return pl.pallas_call(
        kern,
        grid=(t // tm,),
        in_specs=[pl.BlockSpec((tm, d), row)] + [_full_spec(p) for p in params]
        + [pl.BlockSpec((tm, LANES), pos), pl.BlockSpec((tm, LANES), pos)],
        out_specs=[pl.BlockSpec((tm, nh * LANES), row), pl.BlockSpec((tm, nh * LANES), row),
                   pl.BlockSpec((nh * vd, tm), lambda i: (i // n_pos, i % n_pos))],
        out_shape=[jax.ShapeDtypeStruct((t, nh * LANES), BF16),
                   jax.ShapeDtypeStruct((t, nh * LANES), BF16),
                   jax.ShapeDtypeStruct((t // seq_pad * nh * vd, seq_pad), BF16)],
        compiler_params=_params(1),
        name=name,
    )(h, *params, fq, fk)


def _attn_kernel(q_ref, k_ref, vt_ref, o_ref, sa_ref, sb_ref, m_ref, l_ref, acc_ref, *, tq, vd):
    qi = pl.program_id(2)
    m_ref[...] = jnp.full(m_ref.shape, NEG_BIG, F32)
    l_ref[...] = jnp.zeros(l_ref.shape, F32)
    acc_ref[...] = jnp.zeros(acc_ref.shape, F32)

    def scores(kb, s_ref):
        r0 = pl.multiple_of(kb * tq, tq)
        for j in range(2):
            s_ref[j] = _dot_nt(k_ref[pl.ds(r0, tq), j * LANES:(j + 1) * LANES],
                               q_ref[:, j * LANES:(j + 1) * LANES])

    def update(kb, s_ref, masked):
        r0 = pl.multiple_of(kb * tq, tq)
        for j in range(2):
            st = s_ref[j]
            if masked:
                ki = lax.broadcasted_iota(jnp.int32, (tq, tq), 0)
                qj = lax.broadcasted_iota(jnp.int32, (tq, tq), 1)
                st = jnp.where(ki <= qj, st, NEG_BIG)
            m_prev = m_ref[j]
            m_new = jnp.maximum(m_prev, jnp.max(st, axis=0, keepdims=True))
            alpha = jnp.exp2(m_prev - m_new)
            p = jnp.exp2(st - m_new)
            l_ref[j] = alpha * l_ref[j] + jnp.sum(p, axis=0, keepdims=True)
            m_ref[j] = m_new
            vt = vt_ref[j * vd:(j + 1) * vd, pl.ds(r0, tq)]
            acc_ref[j * vd:(j + 1) * vd, :] = (acc_ref[j * vd:(j + 1) * vd, :] * alpha
                                               + _dot(vt, p.astype(BF16)))

    scores(0, sa_ref)

    def pair(p, carry):
        kb = 2 * p
        scores(kb + 1, sb_ref)
        update(kb, sa_ref, False)
        scores(kb + 2, sa_ref)
        update(kb + 1, sb_ref, False)
        return carry

    lax.fori_loop(0, lax.shift_right_logical(qi, 1), pair, 0)

    @pl.when(qi % 2 == 0)
    def _():
        update(qi, sa_ref, True)

    @pl.when(qi % 2 == 1)
    def _():
        scores(qi, sb_ref)
        update(qi - 1, sa_ref, False)
        update(qi, sb_ref, True)

    inv_l = jnp.concatenate([jnp.broadcast_to(1.0 / l_ref[j], (vd, tq)) for j in range(2)], axis=0)
    o_ref[...] = (acc_ref[...] * inv_l).T.astype(o_ref.dtype)


def attention(q, k, vt, *, batch, seq_pad, name):
    t = q.shape[0]
    vd = MLA_V
    n_pairs = q.shape[1] // (2 * LANES)
    tq = _pick_tile(seq_pad, (384, 256, 128))
    nq = seq_pad // tq
    kern = functools.partial(_attn_kernel, tq=tq, vd=vd)
    return pl.pallas_call(
        kern,
        grid=(batch, n_pairs, nq),
        in_specs=[pl.BlockSpec((tq, 2 * LANES), lambda b, p, i: (b * nq + i, p)),
                  pl.BlockSpec((seq_pad, 2 * LANES), lambda b, p, i: (b, p)),
                  pl.BlockSpec((2 * vd, seq_pad), lambda b, p, i: (b * n_pairs + p, 0))],
        out_specs=pl.BlockSpec((tq, 2 * vd), lambda b, p, i: (b * nq + i, p)),
        out_shape=jax.ShapeDtypeStruct((t, n_pairs * 2 * vd), BF16),
        scratch_shapes=[pltpu.VMEM((2, tq, tq), F32), pltpu.VMEM((2, tq, tq), F32),
                        pltpu.VMEM((2, 1, tq), F32), pltpu.VMEM((2, 1, tq), F32),
                        pltpu.VMEM((2 * vd, tq), F32)],
        compiler_params=_params(3),
        name=name,
    )(q, k, vt)


def mla_layer(h, ln_g, w_in, q_a_g, w_q_b, kv_a_g, w_kv_b, q_norm_g, k_norm_g, w_out,
              *, batch, seq_pad, idx):
    q, k, vt = mla_prep(h, ln_g, w_in, q_a_g, w_q_b, kv_a_g, w_kv_b, q_norm_g, k_norm_g,
                        seq_pad=seq_pad, name=f"mla_prep_{idx}")
    o = attention(q, k, vt, batch=batch, seq_pad=seq_pad, name=f"mla_attn_{idx}")
    return matmul_residual(o, w_out.astype(BF16), h, name=f"mla_out_proj_{idx}")


def kernel(x, meta_tokens, ln_mix, ln_mlp, ssd_w_in, ssd_conv_w, ssd_conv_b, ssd_dt_bias, ssd_a_log, ssd_d, ssd_norm, ssd_w_out, mla_w_in, mla_q_a_norm, mla_w_q_b, mla_kv_a_norm, mla_w_kv_b, mla_q_norm, mla_k_norm, mla_w_out, mlp_w_up, mlp_w_down):
    bsz, seq, d = x.shape
    n_meta = meta_tokens.shape[0]
    depth = ln_mix.shape[0]
    seq_all = n_meta + seq
    seq_pad = -(-seq_all // SSD_CHUNK) * SSD_CHUNK
    meta = jnp.broadcast_to(meta_tokens[None].astype(x.dtype), (bsz, n_meta, d))
    h = jnp.concatenate([meta, x, jnp.zeros((bsz, seq_pad - seq_all, d), x.dtype)], axis=1)
    h = h.reshape(bsz * seq_pad, d)
    for i in range(depth):
        j = i // 2
        if i % 2 == 0:
            h = ssd_layer(h, ln_mix[i], ssd_w_in[j], ssd_conv_w[j], ssd_conv_b[j], ssd_dt_bias[j],
                          ssd_a_log[j], ssd_d[j], ssd_norm[j], ssd_w_out[j], batch=bsz, idx=j)
        else:
            h = mla_layer(h, ln_mix[i], mla_w_in[j], mla_q_a_norm[j], mla_w_q_b[j], mla_kv_a_norm[j],
                          mla_w_kv_b[j], mla_q_norm[j], mla_k_norm[j], mla_w_out[j],
                          batch=bsz, seq_pad=seq_pad, idx=j)
        h = mlp_block(h, ln_mlp[i][None], mlp_w_up[i].astype(BF16), mlp_w_down[i].astype(BF16),
                      name=f"mlp_{i}")
    return h.reshape(bsz, seq_pad, d)[:, n_meta:seq_all]
```

```python
import functools

import jax
import jax.numpy as jnp
from jax import lax
from jax.experimental import pallas as pl
from jax.experimental.pallas import tpu as pltpu

F32 = jnp.float32
BF16 = jnp.bfloat16

EPS = 1e-6
ROPE_THETA = 10000.0
N_META = 16
SSD_HEAD_DIM = 64
SSD_STATE = 128
SSD_GROUPS = 8
SSD_CHUNK = 128
MLA_HEADS = 16
MLA_NOPE = 64
MLA_ROPE = 32
MLA_V = 64
MLA_Q_RANK = 384
MLA_KV_RANK = 256

LANES = 128
VMEM_LIMIT_BYTES = 56 * 1024 * 1024
NEG_BIG = -1e30
LOG2_E = 1.4426950408889634


def _params(n_axes):
    return pltpu.CompilerParams(
        dimension_semantics=("arbitrary",) * n_axes,
        vmem_limit_bytes=VMEM_LIMIT_BYTES)


def _pick_tile(n, candidates):
    for c in candidates:
        if n % c == 0:
            return c
    raise ValueError(f"no tile in {candidates} divides {n}")


def _full_spec(a):
    nd = a.ndim
    return pl.BlockSpec(a.shape, lambda *_: (0,) * nd, pipeline_mode=pl.Buffered(1))


def _rms(x, gain):
    ms = jnp.mean(x * x, axis=-1, keepdims=True)
    return x * lax.rsqrt(ms + EPS) * gain


def _silu(x):
    hx = 0.5 * x
    return hx + hx * jnp.tanh(hx)


def _dot(a, b):
    return jnp.dot(a, b, preferred_element_type=F32)


def _dot_nt(a, b):
    return lax.dot_general(a, b, (((1,), (1,)), ((), ())), preferred_element_type=F32)


def _norm_matmul_kernel(h_ref, g_ref, *refs, n_out, col_chunk):
    w_refs, o_refs = refs[:n_out], refs[n_out:]
    hn = _rms(h_ref[...], g_ref[...]).astype(BF16)
    for w_ref, o_ref in zip(w_refs, o_refs):
        n = w_ref.shape[1]
        for c0 in range(0, n, col_chunk):
            c1 = min(c0 + col_chunk, n)
            o_ref[:, c0:c1] = _dot(hn, w_ref[:, c0:c1]).astype(o_ref.dtype)


def norm_matmul(h, gain, weights, out_dtypes, *, name):
    t, d = h.shape
    tm = _pick_tile(t, (256, 128))
    kern = functools.partial(_norm_matmul_kernel, n_out=len(weights), col_chunk=1024)
    return pl.pallas_call(
        kern,
        grid=(t // tm,),
        in_specs=[pl.BlockSpec((tm, d), lambda i: (i, 0)), _full_spec(gain)]
        + [_full_spec(w) for w in weights],
        out_specs=[pl.BlockSpec((tm, w.shape[1]), lambda i: (i, 0)) for w in weights],
        out_shape=[jax.ShapeDtypeStruct((t, w.shape[1]), dt) for w, dt in zip(weights, out_dtypes)],
        compiler_params=_params(1),
        name=name,
    )(h, gain, *weights)


def _proj_mlp_kernel(a_ref, wo_ref, h_ref, g_ref, wu_ref, wd_ref, o_ref, *, ff_chunk):
    x = h_ref[...] + _dot(a_ref[...], wo_ref[...])
    hn = _rms(x, g_ref[...]).astype(BF16)
    acc = x
    for c0 in range(0, wu_ref.shape[1], ff_chunk):
        u = jnp.maximum(_dot(hn, wu_ref[:, c0:c0 + ff_chunk]), 0.0)
        acc = acc + _dot((u * u).astype(BF16), wd_ref[c0:c0 + ff_chunk, :])
    o_ref[...] = acc


def proj_mlp_block(a, w_out, h, gain, w_up, w_down, *, name):
    t, d = h.shape
    k = a.shape[1]
    tm = _pick_tile(t, (512, 256, 128))
    kern = functools.partial(_proj_mlp_kernel, ff_chunk=1024)
    row = lambda i: (i, 0)
    return pl.pallas_call(
        kern,
        grid=(t // tm,),
        in_specs=[pl.BlockSpec((tm, k), row), _full_spec(w_out), pl.BlockSpec((tm, d), row),
                  _full_spec(gain), _full_spec(w_up), _full_spec(w_down)],
        out_specs=pl.BlockSpec((tm, d), row),
        out_shape=jax.ShapeDtypeStruct((t, d), F32),
        input_output_aliases={2: 0},
        compiler_params=_params(1),
        name=name,
    )(a, w_out, h, gain, w_up, w_down)


def _split3_bf16(x):
    p1 = x.astype(BF16)
    r1 = x - p1.astype(F32)
    p2 = r1.astype(BF16)
    p3 = (r1 - p2.astype(F32)).astype(BF16)
    return p1, p2, p3


def _ssd_kernel(z_ref, xbc_ref, dt_ref, cw_ref, cb_ref, dtb_ref, alog_ref, dsk_ref, ng_ref,
                o_ref, ext_ref, xc_ref, st_ref, acs_ref, acst_ref, dts_ref,
                *, n_groups, hpg, hdim, n_state, chunk, conv_k):
    c = pl.program_id(1)
    gw = hpg * hdim
    d_inner = n_groups * gw
    conv_dim = d_inner + 2 * n_groups * n_state
    tail = 8

    @pl.when(c == 0)
    def _():
        ext_ref[0:tail, :] = jnp.zeros((tail, conv_dim), F32)
        st_ref[...] = jnp.zeros_like(st_ref)

    @pl.when(c != 0)
    def _():
        ext_ref[0:tail, :] = ext_ref[chunk:chunk + tail, :]

    ext_ref[tail:tail + chunk, :] = xbc_ref[...].astype(F32)

    cc = 512
    for c0 in range(0, conv_dim, cc):
        acc = cb_ref[:, c0:c0 + cc] + cw_ref[conv_k - 1:conv_k, c0:c0 + cc] * ext_ref[tail:tail + chunk, c0:c0 + cc]
        for k in range(conv_k - 1):
            r0 = tail - (conv_k - 1) + k
            acc = acc + cw_ref[k:k + 1, c0:c0 + cc] * ext_ref[r0:r0 + chunk, c0:c0 + cc]
        xc_ref[:, c0:c0 + cc] = _silu(acc)

    dtr = dt_ref[...] + dtb_ref[...]
    dt = jnp.maximum(dtr, 0.0) + jnp.log(1.0 + jnp.exp(-jnp.abs(dtr)))
    dts_ref[...] = dt
    a_dt = dt * (-LOG2_E * jnp.exp(alog_ref[...]))
    row_i = lax.broadcasted_iota(jnp.int32, (chunk, chunk), 0)
    col_i = lax.broadcasted_iota(jnp.int32, (chunk, chunk), 1)
    causal = row_i >= col_i
    tri = jnp.where(causal, 1.0, 0.0).astype(BF16)
    p1, p2, p3 = _split3_bf16(a_dt)
    a_cs = _dot(tri, p1) + _dot(tri, p2) + _dot(tri, p3)
    acs_ref[...] = a_cs
    acst_ref[...] = a_cs.T

    lane = lax.broadcasted_iota(jnp.int32, (chunk, LANES), 1)
    first_half = lane < hdim
    b_off = d_inner
    c_off = d_inner + n_groups * n_state

    for g in range(n_groups):
        xs_g = xc_ref[:, g * gw:(g + 1) * gw]
        b_f32 = xc_ref[:, b_off + g * n_state:b_off + (g + 1) * n_state]
        b_g = b_f32.astype(BF16)
        c_g = xc_ref[:, c_off + g * n_state:c_off + (g + 1) * n_state].astype(BF16)
        cb = _dot_nt(c_g, b_g)
        prev_t = st_ref[g]
        y_off = _dot(c_g, prev_t.astype(BF16))
        y_parts, acs_parts, xdt_parts = [], [], []
        for pr in range(hpg // 2):
            heads = (g * hpg + 2 * pr, g * hpg + 2 * pr + 1)
            acs_p = jnp.where(first_half, acs_ref[:, heads[0]:heads[0] + 1],
                              acs_ref[:, heads[1]:heads[1] + 1])
            dt_p = jnp.where(first_half, dts_ref[:, heads[0]:heads[0] + 1],
                             dts_ref[:, heads[1]:heads[1] + 1])
            xdt_p = xs_g[:, pr * LANES:(pr + 1) * LANES] * dt_p
            xdt_bf = xdt_p.astype(BF16)
            prods = []
            for hh in heads:
                decay = jnp.exp2(jnp.where(causal, acs_ref[:, hh:hh + 1] - acst_ref[hh:hh + 1, :], -jnp.inf))
                prods.append(_dot((cb * decay).astype(BF16), xdt_bf))
            y_parts.append(jnp.where(first_half, prods[0], prods[1]))
            acs_parts.append(acs_p)
            xdt_parts.append(xdt_p)
        y_diag = jnp.concatenate(y_parts, axis=1)
        acs_x = jnp.concatenate(acs_parts, axis=1)
        xdt = jnp.concatenate(xdt_parts, axis=1)
        a_last = acs_x[chunk - 1:chunk, :]
        y = y_diag + y_off * jnp.exp2(acs_x) + xs_g * dsk_ref[:, g * gw:(g + 1) * gw]
        xdtd = (xdt * jnp.exp2(a_last - acs_x)).astype(BF16)
        st_ref[g] = prev_t * jnp.exp2(a_last) + _dot(b_f32.T.astype(BF16), xdtd)
        zg = z_ref[:, g * gw:(g + 1) * gw].astype(F32)
        gated = y * _silu(zg)
        o_ref[:, g * gw:(g + 1) * gw] = _rms(gated, ng_ref[:, g * gw:(g + 1) * gw]).astype(o_ref.dtype)


def ssd_core(z, xbc, dt_raw, conv_w, conv_b, dt_bias, a_log, d_skip, norm_g, *, batch, name):
    t, d_inner = z.shape
    conv_dim = xbc.shape[1]
    n_state = SSD_STATE
    n_groups = (conv_dim - d_inner) // (2 * n_state)
    hdim = SSD_HEAD_DIM
    hpg = d_inner // (n_groups * hdim)
    assert 2 * hdim == LANES and hpg % 2 == 0
    chunk = SSD_CHUNK
    nc = t // (batch * chunk)
    conv_k = conv_w.shape[0]
    kern = functools.partial(_ssd_kernel, n_groups=n_groups, hpg=hpg, hdim=hdim,
                             n_state=n_state, chunk=chunk, conv_k=conv_k)
    row = lambda b, c: (b * nc + c, 0)
    return pl.pallas_call(
        kern,
        grid=(batch, nc),
        in_specs=[pl.BlockSpec((chunk, d_inner), row),
                  pl.BlockSpec((chunk, conv_dim), row),
                  pl.BlockSpec((chunk, LANES), row),
                  _full_spec(conv_w), _full_spec(conv_b), _full_spec(dt_bias),
                  _full_spec(a_log), _full_spec(d_skip), _full_spec(norm_g)],
        out_specs=pl.BlockSpec((chunk, d_inner), row),
        out_shape=jax.ShapeDtypeStruct((t, d_inner), BF16),
        scratch_shapes=[
            pltpu.VMEM((chunk + 8, conv_dim), F32),
            pltpu.VMEM((chunk, conv_dim), F32),
            pltpu.VMEM((n_groups, n_state, hpg * hdim), F32),
            pltpu.VMEM((chunk, LANES), F32),
            pltpu.VMEM((LANES, chunk), F32),
            pltpu.VMEM((chunk, LANES), F32),
        ],
        compiler_params=_params(2),
        name=name,
    )(z, xbc, dt_raw, conv_w, conv_b, dt_bias, a_log, d_skip, norm_g)


def _pad_cols(a, n):
    return jnp.pad(a, [(0, 0)] * (a.ndim - 1) + [(0, n - a.shape[-1])])


def ssd_mixer(h, ln_g, w_in, conv_w, conv_b, dt_bias, a_log, d_skip, norm_g, *, batch, idx):
    d_inner = norm_g.shape[0]
    n_heads = dt_bias.shape[0]
    conv_dim = conv_w.shape[1]
    w_z = w_in[:, :d_inner].astype(BF16)
    w_xbc = w_in[:, d_inner:d_inner + conv_dim].astype(BF16)
    w_dt = _pad_cols(w_in[:, d_inner + conv_dim:], LANES).astype(BF16)
    z, xbc, dt_raw = norm_matmul(h, ln_g[None], [w_z, w_xbc, w_dt], [BF16, BF16, F32],
                                 name=f"ssd_in_proj_{idx}")
    return ssd_core(z, xbc, dt_raw, conv_w, conv_b[None], _pad_cols(dt_bias[None], LANES),
                    _pad_cols(a_log[None], LANES), jnp.repeat(d_skip, d_inner // n_heads)[None],
                    norm_g[None], batch=batch, name=f"ssd_core_{idx}")


def _mla_prep_kernel(h_ref, g_ref, win_ref, qag_ref, wq_ref, kvag_ref, wk_ref, wv_ref,
                     qg_ref, kg_ref, kpeg_ref, fq_ref, fk_ref, q_out, k_out, v_out,
                     *, n_heads, q_rank, kv_rank, qk_dim, rope_dim):
    hn = _rms(h_ref[...], g_ref[...]).astype(BF16)
    lat = _dot(hn, win_ref[...])
    qn = _rms(lat[:, :q_rank], qag_ref[...]).astype(BF16)
    kvn = _rms(lat[:, q_rank:q_rank + kv_rank], kvag_ref[...]).astype(BF16)
    kpe = lat[:, q_rank + kv_rank:]
    tm = kpe.shape[0]
    lane = lax.broadcasted_iota(jnp.int32, (tm, LANES), 1)

    v_out[...] = _dot_nt(wv_ref[...], kvn).astype(v_out.dtype)

    ss_pe = jnp.sum(jnp.where(lane < rope_dim, kpe * kpe, 0.0), axis=-1, keepdims=True)
    rot = kpe * kpeg_ref[...] * fk_ref[...]
    rot = rot + pltpu.roll(rot, rope_dim, 1)
    kf = jnp.where(lane >= LANES - 2 * rope_dim, rot, 0.0)

    k_raw = _dot(kvn, wk_ref[...])
    kg = kg_ref[...]
    for hh in range(n_heads):
        kh = k_raw[:, hh * LANES:(hh + 1) * LANES]
        ss = jnp.sum(kh * kh, axis=-1, keepdims=True) + ss_pe
        r = lax.rsqrt(ss * (1.0 / qk_dim) + EPS)
        k_out[:, hh * LANES:(hh + 1) * LANES] = ((kh * kg + kf) * r).astype(k_out.dtype)

    q_raw = _dot(qn, wq_ref[...])
    gf = qg_ref[...] * fq_ref[...]
    for hh in range(n_heads):
        qh = q_raw[:, hh * LANES:(hh + 1) * LANES]
        ss = jnp.sum(jnp.where(lane < qk_dim, qh * qh, 0.0), axis=-1, keepdims=True)
        r = lax.rsqrt(ss * (1.0 / qk_dim) + EPS)
        q_out[:, hh * LANES:(hh + 1) * LANES] = (qh * gf * r).astype(q_out.dtype)


def _swap_halves(a):
    n = a.shape[-1] // 2
    return jnp.concatenate([a[..., n:], a[..., :n]], axis=-1)


def mla_prep(h, ln_g, w_in, q_a_g, w_q_b, kv_a_g, w_kv_b, q_norm_g, k_norm_g, *, seq_pad, name):
    t, d = h.shape
    nh, nope, rope, vd = MLA_HEADS, MLA_NOPE, MLA_ROPE, MLA_V
    qk = nope + rope
    q_rank, kv_rank = q_a_g.shape[0], kv_a_g.shape[0]
    assert nope + 2 * rope == LANES
    tm = _pick_tile(seq_pad, (384, 256, 128))

    kpe_w = w_in[:, q_rank + kv_rank:]
    kpe_blk = jnp.concatenate([kpe_w, _swap_halves(kpe_w)] * 2, axis=1)
    w_in_l = jnp.concatenate([w_in[:, :q_rank + kv_rank], kpe_blk], axis=1).astype(BF16)

    wq = w_q_b.reshape(q_rank, nh, qk)
    wq_l = jnp.concatenate([wq, _swap_halves(wq[..., nope:])], axis=-1).reshape(q_rank, nh * LANES).astype(BF16)
    wkv = w_kv_b.reshape(kv_rank, nh, nope + vd)
    wk_l = _pad_cols(wkv[..., :nope], LANES).reshape(kv_rank, nh * LANES).astype(BF16)
    wv_l = wkv[..., nope:].reshape(kv_rank, nh * vd).T.astype(BF16)

    scale = qk ** -0.5 * LOG2_E
    qg_l = (jnp.concatenate([q_norm_g, _swap_halves(q_norm_g[nope:])]) * scale)[None]
    kg_l = _pad_cols(k_norm_g[:nope], LANES)[None]
    kpeg_l = jnp.concatenate([k_norm_g[nope:], _swap_halves(k_norm_g[nope:])] * 2)[None]

    inv = 1.0 / (ROPE_THETA ** (jnp.arange(0, rope, 2, dtype=F32) / rope))
    ang = jnp.arange(seq_pad, dtype=F32)[:, None] * inv[None, :]
    cos, sin = jnp.cos(ang), jnp.sin(ang)
    rot_f = jnp.concatenate([cos, cos, -sin, sin], axis=1)
    fq = jnp.concatenate([jnp.ones((seq_pad, nope), F32), rot_f], axis=1)
    fk = jnp.concatenate([rot_f, rot_f], axis=1)

    n_pos = seq_pad // tm
    kern = functools.partial(_mla_prep_kernel, n_heads=nh, q_rank=q_rank, kv_rank=kv_rank,
                             qk_dim=qk, rope_dim=rope)
    row = lambda i: (i, 0)
    pos = lambda i: (i % n_pos, 0)
    params = [ln_g[None], w_in_l, q_a_g[None], wq_l, kv_a_g[None], wk_l, wv_l, qg_l, kg_l, kpeg_l]
    return pl.pallas_call(
        kern,
        grid=(t // tm,),
        in_specs=[pl.BlockSpec((tm, d), row)] + [_full_spec(p) for p in params]
        + [pl.BlockSpec((tm, LANES), pos), pl.BlockSpec((tm, LANES), pos)],
        out_specs=[pl.BlockSpec((tm, nh * LANES), row), pl.BlockSpec((tm, nh * LANES), row),
                   pl.BlockSpec((nh * vd, tm), lambda i: (i // n_pos, i % n_pos))],
        out_shape=[jax.ShapeDtypeStruct((t, nh * LANES), BF16),
                   jax.ShapeDtypeStruct((t, nh * LANES), BF16),
                   jax.ShapeDtypeStruct((t // seq_pad * nh * vd, seq_pad), BF16)],
        compiler_params=_params(1),
        name=name,
    )(h, *params, fq, fk)


def _attn_kernel(q_ref, k_ref, vt_ref, o_ref, sa_ref, sb_ref, m_ref, l_ref, acc_ref, *, tq, vd):
    qi = pl.program_id(2)
    m_ref[...] = jnp.full(m_ref.shape, NEG_BIG, F32)
    l_ref[...] = jnp.zeros(l_ref.shape, F32)
    acc_ref[...] = jnp.zeros(acc_ref.shape, F32)

    def scores(kb, s_ref):
        r0 = pl.multiple_of(kb * tq, tq)
        for j in range(2):
            s_ref[j] = _dot_nt(k_ref[pl.ds(r0, tq), j * LANES:(j + 1) * LANES],
                               q_ref[:, j * LANES:(j + 1) * LANES])

    def update(kb, s_ref, masked):
        r0 = pl.multiple_of(kb * tq, tq)
        for j in range(2):
            st = s_ref[j]
            if masked:
                ki = lax.broadcasted_iota(jnp.int32, (tq, tq), 0)
                qj = lax.broadcasted_iota(jnp.int32, (tq, tq), 1)
                st = jnp.where(ki <= qj, st, NEG_BIG)
            m_prev = m_ref[j]
            m_new = jnp.maximum(m_prev, jnp.max(st, axis=0, keepdims=True))
            alpha = jnp.exp2(m_prev - m_new)
            p = jnp.exp2(st - m_new)
            l_ref[j] = alpha * l_ref[j] + jnp.sum(p, axis=0, keepdims=True)
            m_ref[j] = m_new
            vt = vt_ref[j * vd:(j + 1) * vd, pl.ds(r0, tq)]
            acc_ref[j * vd:(j + 1) * vd, :] = (acc_ref[j * vd:(j + 1) * vd, :] * alpha
                                               + _dot(vt, p.astype(BF16)))

    bufs = (sa_ref, sb_ref)
    unroll = 4
    scores(0, sa_ref)

    def body(p, carry):
        kb = unroll * p
        for t in range(unroll):
            scores(kb + t + 1, bufs[(t + 1) % 2])
            update(kb + t, bufs[t % 2], False)
        return carry

    lax.fori_loop(0, lax.shift_right_logical(qi, unroll.bit_length() - 1), body, 0)

    rem = jnp.bitwise_and(qi, unroll - 1)
    k0 = qi - rem
    for r in range(unroll):
        @pl.when(rem == r)
        def _(r=r):
            for t in range(r):
                scores(k0 + t + 1, bufs[(t + 1) % 2])
                update(k0 + t, bufs[t % 2], False)
            update(k0 + r, bufs[r % 2], True)

    inv_l = jnp.concatenate([jnp.broadcast_to(1.0 / l_ref[j], (vd, tq)) for j in range(2)], axis=0)
    o_ref[...] = (acc_ref[...] * inv_l).T.astype(o_ref.dtype)


def attention(q, k, vt, *, batch, seq_pad, name):
    t = q.shape[0]
    vd = MLA_V
    n_pairs = q.shape[1] // (2 * LANES)
    tq = _pick_tile(seq_pad, (384, 256, 128))
    nq = seq_pad // tq
    kern = functools.partial(_attn_kernel, tq=tq, vd=vd)
    return pl.pallas_call(
        kern,
        grid=(batch, n_pairs, nq),
        in_specs=[pl.BlockSpec((tq, 2 * LANES), lambda b, p, i: (b * nq + i, p)),
                  pl.BlockSpec((seq_pad, 2 * LANES), lambda b, p, i: (b, p)),
                  pl.BlockSpec((2 * vd, seq_pad), lambda b, p, i: (b * n_pairs + p, 0))],
        out_specs=pl.BlockSpec((tq, 2 * vd), lambda b, p, i: (b * nq + i, p)),
        out_shape=jax.ShapeDtypeStruct((t, n_pairs * 2 * vd), BF16),
        scratch_shapes=[pltpu.VMEM((2, tq, tq), F32), pltpu.VMEM((2, tq, tq), F32),
                        pltpu.VMEM((2, 1, tq), F32), pltpu.VMEM((2, 1, tq), F32),
                        pltpu.VMEM((2 * vd, tq), F32)],
        compiler_params=_params(3),
        name=name,
    )(q, k, vt)


def mla_mixer(h, ln_g, w_in, q_a_g, w_q_b, kv_a_g, w_kv_b, q_norm_g, k_norm_g, *, batch, seq_pad, idx):
    q, k, vt = mla_prep(h, ln_g, w_in, q_a_g, w_q_b, kv_a_g, w_kv_b, q_norm_g, k_norm_g,
                        seq_pad=seq_pad, name=f"mla_prep_{idx}")
    return attention(q, k, vt, batch=batch, seq_pad=seq_pad, name=f"mla_attn_{idx}")


def kernel(x, meta_tokens, ln_mix, ln_mlp, ssd_w_in, ssd_conv_w, ssd_conv_b, ssd_dt_bias, ssd_a_log, ssd_d, ssd_norm, ssd_w_out, mla_w_in, mla_q_a_norm, mla_w_q_b, mla_kv_a_norm, mla_w_kv_b, mla_q_norm, mla_k_norm, mla_w_out, mlp_w_up, mlp_w_down):
    bsz, seq, d = x.shape
    n_meta = meta_tokens.shape[0]
    depth = ln_mix.shape[0]
    seq_all = n_meta + seq
    seq_pad = -(-seq_all // SSD_CHUNK) * SSD_CHUNK
    meta = jnp.broadcast_to(meta_tokens[None].astype(x.dtype), (bsz, n_meta, d))
    h = jnp.concatenate([meta, x, jnp.zeros((bsz, seq_pad - seq_all, d), x.dtype)], axis=1)
    h = h.reshape(bsz * seq_pad, d)
    for i in range(depth):
        j = i // 2
        if i % 2 == 0:
            a = ssd_mixer(h, ln_mix[i], ssd_w_in[j], ssd_conv_w[j], ssd_conv_b[j], ssd_dt_bias[j],
                          ssd_a_log[j], ssd_d[j], ssd_norm[j], batch=bsz, idx=j)
            w_out = ssd_w_out[j]
        else:
            a = mla_mixer(h, ln_mix[i], mla_w_in[j], mla_q_a_norm[j], mla_w_q_b[j], mla_kv_a_norm[j],
                          mla_w_kv_b[j], mla_q_norm[j], mla_k_norm[j],
                          batch=bsz, seq_pad=seq_pad, idx=j)
            w_out = mla_w_out[j]
        h = proj_mlp_block(a, w_out.astype(BF16), h, ln_mlp[i][None], mlp_w_up[i].astype(BF16),
                           mlp_w_down[i].astype(BF16), name=f"proj_mlp_{i}")
    return h.reshape(bsz, seq_pad, d)[:, n_meta:seq_all]
```

```python
import functools

import jax
import jax.numpy as jnp
from jax import lax
from jax.experimental import pallas as pl
from jax.experimental.pallas import tpu as pltpu

F32 = jnp.float32
BF16 = jnp.bfloat16

EPS = 1e-6
ROPE_THETA = 10000.0
N_META = 16
SSD_HEAD_DIM = 64
SSD_STATE = 128
SSD_GROUPS = 8
SSD_CHUNK = 128
MLA_HEADS = 16
MLA_NOPE = 64
MLA_ROPE = 32
MLA_V = 64
MLA_Q_RANK = 384
MLA_KV_RANK = 256

LANES = 128
VMEM_LIMIT_BYTES = 56 * 1024 * 1024
NEG_BIG = -1e30
LOG2_E = 1.4426950408889634


def _params(n_axes):
    return pltpu.CompilerParams(
        dimension_semantics=("arbitrary",) * n_axes,
        vmem_limit_bytes=VMEM_LIMIT_BYTES)


def _pick_tile(n, candidates):
    for c in candidates:
        if n % c == 0:
            return c
    raise ValueError(f"no tile in {candidates} divides {n}")


def _full_spec(a):
    nd = a.ndim
    return pl.BlockSpec(a.shape, lambda *_: (0,) * nd, pipeline_mode=pl.Buffered(1))


def _rms(x, gain):
    ms = jnp.mean(x * x, axis=-1, keepdims=True)
    return x * lax.rsqrt(ms + EPS) * gain


def _silu(x):
    hx = 0.5 * x
    return hx + hx * jnp.tanh(hx)


def _dot(a, b):
    return jnp.dot(a, b, preferred_element_type=F32)


def _dot_nt(a, b):
    return lax.dot_general(a, b, (((1,), (1,)), ((), ())), preferred_element_type=F32)


def _proj_mlp_kernel(a_ref, wo_ref, h_ref, g_ref, wu_ref, wd_ref, o_ref, *, ff_chunk):
    x = h_ref[...] + _dot(a_ref[...], wo_ref[...])
    hn = _rms(x, g_ref[...]).astype(BF16)
    acc = x
    for c0 in range(0, wu_ref.shape[1], ff_chunk):
        u = jnp.maximum(_dot(hn, wu_ref[:, c0:c0 + ff_chunk]), 0.0)
        acc = acc + _dot((u * u).astype(BF16), wd_ref[c0:c0 + ff_chunk, :])
    o_ref[...] = acc


def proj_mlp_block(a, w_out, h, gain, w_up, w_down, *, name):
    t, d = h.shape
    k = a.shape[1]
    tm = _pick_tile(t, (512, 256, 128))
    kern = functools.partial(_proj_mlp_kernel, ff_chunk=1024)
    row = lambda i: (i, 0)
    return pl.pallas_call(
        kern,
        grid=(t // tm,),
        in_specs=[pl.BlockSpec((tm, k), row), _full_spec(w_out), pl.BlockSpec((tm, d), row),
                  _full_spec(gain), _full_spec(w_up), _full_spec(w_down)],
        out_specs=pl.BlockSpec((tm, d), row),
        out_shape=jax.ShapeDtypeStruct((t, d), F32),
        input_output_aliases={2: 0},
        compiler_params=_params(1),
        name=name,
    )(a, w_out, h, gain, w_up, w_down)


CONV_TAIL = 8


def _split3_bf16(x):
    p1 = x.astype(BF16)
    r1 = x - p1.astype(F32)
    p2 = r1.astype(BF16)
    p3 = (r1 - p2.astype(F32)).astype(BF16)
    return p1, p2, p3


def _ssd_kernel(h0_ref, ha_ref, hb_ref, g_ref, wz_ref, wx_ref, wdt_ref, cw_ref, cb_ref, dtb_ref,
                alog_ref, dsk_ref, ng_ref, o_ref,
                exta_ref, extb_ref, za_ref, zb_ref, dta_ref, dtb2_ref, hn_ref, xc_ref, st_ref,
                acs_ref, acst_ref, dts_ref,
                *, n_groups, hpg, hdim, n_state, chunk, chunks_per_batch, conv_chunk):
    i = pl.program_id(0)
    gw = hpg * hdim
    d_inner = n_groups * gw
    conv_dim = d_inner + 2 * n_groups * n_state
    conv_k = cw_ref.shape[0]
    slots = ((exta_ref, za_ref, dta_ref), (extb_ref, zb_ref, dtb2_ref))
    x_chunks = list(range(0, conv_dim, conv_chunk))

    row_i = lax.broadcasted_iota(jnp.int32, (chunk, chunk), 0)
    col_i = lax.broadcasted_iota(jnp.int32, (chunk, chunk), 1)
    causal = row_i >= col_i
    tri = jnp.where(causal, 1.0, 0.0).astype(BF16)
    lane = lax.broadcasted_iota(jnp.int32, (chunk, LANES), 1)
    first_half = lane < hdim
    row8 = lax.broadcasted_iota(jnp.int32, (1, CONV_TAIL, conv_chunk), 1)
    b_off = d_inner
    c_off = d_inner + n_groups * n_state

    def project_x(slot, c0):
        slots[slot][0][CONV_TAIL:CONV_TAIL + chunk, c0:c0 + conv_chunk] = _dot(
            hn_ref[...], wx_ref[:, c0:c0 + conv_chunk])

    def project_z(slot, g):
        slots[slot][1][:, g * gw:(g + 1) * gw] = _dot(
            hn_ref[...], wz_ref[:, g * gw:(g + 1) * gw]).astype(BF16)

    def project_dt(slot):
        slots[slot][2][...] = _dot(hn_ref[...], wdt_ref[...])

    def conv(slot, c0):
        ext = slots[slot][0]
        cols = slice(c0, c0 + conv_chunk)
        u = ext[CONV_TAIL:CONV_TAIL + chunk, cols]
        u3 = u.reshape(chunk // CONV_TAIL, CONV_TAIL, conv_chunk)
        prev = ext[0:CONV_TAIL, cols][None]
        acc = cb_ref[:, cols] + cw_ref[conv_k - 1:conv_k, cols] * u
        for k in range(conv_k - 1):
            s = conv_k - 1 - k
            rot = pltpu.roll(u3, s, 1)
            above = jnp.concatenate([pltpu.roll(prev, s, 1), rot[:-1]], axis=0)
            shifted = jnp.where(row8 < s, above, rot).reshape(chunk, conv_chunk)
            acc = acc + cw_ref[k:k + 1, cols] * shifted
        xc_ref[:, cols] = _silu(acc)

    def decay_setup(slot):
        dtr = slots[slot][2][...] + dtb_ref[...]
        dt = jnp.maximum(dtr, 0.0) + jnp.log(1.0 + jnp.exp(-jnp.abs(dtr)))
        dts_ref[...] = dt
        a_dt = dt * (-LOG2_E * jnp.exp(alog_ref[...]))
        p1, p2, p3 = _split3_bf16(a_dt)
        a_cs = _dot(tri, p1) + _dot(tri, p2) + _dot(tri, p3)
        acs_ref[...] = a_cs
        acst_ref[...] = a_cs.T

    def group(slot, g, row0):
        z_ref = slots[slot][1]
        xs_g = xc_ref[:, g * gw:(g + 1) * gw]
        b_f32 = xc_ref[:, b_off + g * n_state:b_off + (g + 1) * n_state]
        b_g = b_f32.astype(BF16)
        c_g = xc_ref[:, c_off + g * n_state:c_off + (g + 1) * n_state].astype(BF16)
        cb = _dot_nt(c_g, b_g)
        prev_t = st_ref[g]
        y_off = _dot(c_g, prev_t.astype(BF16))
        y_parts, acs_parts, xdt_parts = [], [], []
        for pr in range(hpg // 2):
            heads = (g * hpg + 2 * pr, g * hpg + 2 * pr + 1)
            acs_p = jnp.where(first_half, acs_ref[:, heads[0]:heads[0] + 1],
                              acs_ref[:, heads[1]:heads[1] + 1])
            dt_p = jnp.where(first_half, dts_ref[:, heads[0]:heads[0] + 1],
                             dts_ref[:, heads[1]:heads[1] + 1])
            xdt_p = xs_g[:, pr * LANES:(pr + 1) * LANES] * dt_p
            xdt_bf = xdt_p.astype(BF16)
            prods = []
            for hh in heads:
                decay = jnp.exp2(jnp.where(causal, acs_ref[:, hh:hh + 1] - acst_ref[hh:hh + 1, :], -jnp.inf))
                prods.append(_dot((cb * decay).astype(BF16), xdt_bf))
            y_parts.append(jnp.where(first_half, prods[0], prods[1]))
            acs_parts.append(acs_p)
            xdt_parts.append(xdt_p)
        y_diag = jnp.concatenate(y_parts, axis=1)
        acs_x = jnp.concatenate(acs_parts, axis=1)
        xdt = jnp.concatenate(xdt_parts, axis=1)
        a_last = acs_x[chunk - 1:chunk, :]
        y = y_diag + y_off * jnp.exp2(acs_x) + xs_g * dsk_ref[:, g * gw:(g + 1) * gw]
        xdtd = (xdt * jnp.exp2(a_last - acs_x)).astype(BF16)
        st_ref[g] = prev_t * jnp.exp2(a_last) + _dot(b_f32.T.astype(BF16), xdtd)
        zg = z_ref[:, g * gw:(g + 1) * gw].astype(F32)
        gated = y * _silu(zg)
        o_ref[row0:row0 + chunk, g * gw:(g + 1) * gw] = _rms(
            gated, ng_ref[:, g * gw:(g + 1) * gw]).astype(o_ref.dtype)

    def half(cur, nxt, h_next_ref, cidx, row0):
        ext_c, ext_n = slots[cur][0], slots[nxt][0]
        first = cidx % chunks_per_batch == 0

        @pl.when(first)
        def _():
            ext_c[0:CONV_TAIL, :] = jnp.zeros((CONV_TAIL, conv_dim), F32)
            st_ref[...] = jnp.zeros_like(st_ref)

        @pl.when(jnp.logical_not(first))
        def _():
            ext_c[0:CONV_TAIL, :] = ext_n[chunk:chunk + CONV_TAIL, :]

        hn_ref[...] = _rms(h_next_ref[...], g_ref[...]).astype(BF16)
        for c0 in x_chunks:
            project_x(nxt, c0)
            conv(cur, c0)
        project_dt(nxt)
        decay_setup(cur)
        for g in range(n_groups):
            project_z(nxt, g)
            group(cur, g, row0)

    @pl.when(i == 0)
    def _():
        hn_ref[...] = _rms(h0_ref[...], g_ref[...]).astype(BF16)
        for c0 in x_chunks:
            project_x(0, c0)
        project_dt(0)
        for g in range(n_groups):
            project_z(0, g)

    half(0, 1, ha_ref, 2 * i, 0)
    half(1, 0, hb_ref, 2 * i + 1, chunk)


def ssd_mixer(h, ln_g, w_in, conv_w, conv_b, dt_bias, a_log, d_skip, norm_g, *, seq_pad, idx):
    t, d = h.shape
    d_inner = norm_g.shape[0]
    n_heads = dt_bias.shape[0]
    conv_dim = conv_w.shape[1]
    n_state, hdim, chunk = SSD_STATE, SSD_HEAD_DIM, SSD_CHUNK
    n_groups = (conv_dim - d_inner) // (2 * n_state)
    hpg = d_inner // (n_groups * hdim)
    assert 2 * hdim == LANES and hpg % 2 == 0 and conv_w.shape[0] - 1 <= CONV_TAIL
    n_chunks = t // chunk
    assert n_chunks % 2 == 0 and seq_pad % chunk == 0

    w_z = w_in[:, :d_inner].astype(BF16)
    w_xbc = w_in[:, d_inner:d_inner + conv_dim].astype(BF16)
    w_dt = _pad_cols(w_in[:, d_inner + conv_dim:], LANES).astype(BF16)
    params = [ln_g[None], w_z, w_xbc, w_dt, conv_w, conv_b[None], _pad_cols(dt_bias[None], LANES),
              _pad_cols(a_log[None], LANES), jnp.repeat(d_skip, d_inner // n_heads)[None], norm_g[None]]

    kern = functools.partial(_ssd_kernel, n_groups=n_groups, hpg=hpg, hdim=hdim, n_state=n_state,
                             chunk=chunk, chunks_per_batch=seq_pad // chunk, conv_chunk=512)
    return pl.pallas_call(
        kern,
        grid=(n_chunks // 2,),
        in_specs=[pl.BlockSpec((chunk, d), lambda i: (0, 0)),
                  pl.BlockSpec((chunk, d), lambda i: (2 * i + 1, 0)),
                  pl.BlockSpec((chunk, d), lambda i: (jnp.minimum(2 * i + 2, n_chunks - 1), 0))]
        + [_full_spec(p) for p in params],
        out_specs=pl.BlockSpec((2 * chunk, d_inner), lambda i: (i, 0)),
        out_shape=jax.ShapeDtypeStruct((t, d_inner), BF16),
        scratch_shapes=[
            pltpu.VMEM((CONV_TAIL + chunk, conv_dim), F32),
            pltpu.VMEM((CONV_TAIL + chunk, conv_dim), F32),
            pltpu.VMEM((chunk, d_inner), BF16),
            pltpu.VMEM((chunk, d_inner), BF16),
            pltpu.VMEM((chunk, LANES), F32),
            pltpu.VMEM((chunk, LANES), F32),
            pltpu.VMEM((chunk, d), BF16),
            pltpu.VMEM((chunk, conv_dim), F32),
            pltpu.VMEM((n_groups, n_state, hpg * hdim), F32),
            pltpu.VMEM((chunk, LANES), F32),
            pltpu.VMEM((LANES, chunk), F32),
            pltpu.VMEM((chunk, LANES), F32),
        ],
        compiler_params=_params(1),
        name=f"ssd_mixer_{idx}",
    )(h, h, h, *params)


def _pad_cols(a, n):
    return jnp.pad(a, [(0, 0)] * (a.ndim - 1) + [(0, n - a.shape[-1])])


def _mla_prep_kernel(h_ref, g_ref, win_ref, qag_ref, wq_ref, kvag_ref, wk_ref, wv_ref,
                     qg_ref, kg_ref, kpeg_ref, fq_ref, fk_ref, q_out, k_out, v_out,
                     *, n_heads, q_rank, kv_rank, qk_dim, rope_dim):
    hn = _rms(h_ref[...], g_ref[...]).astype(BF16)
    lat = _dot(hn, win_ref[...])
    qn = _rms(lat[:, :q_rank], qag_ref[...]).astype(BF16)
    kvn = _rms(lat[:, q_rank:q_rank + kv_rank], kvag_ref[...]).astype(BF16)
    kpe = lat[:, q_rank + kv_rank:]
    tm = kpe.shape[0]
    lane = lax.broadcasted_iota(jnp.int32, (tm, LANES), 1)

    v_out[...] = _dot_nt(wv_ref[...], kvn).astype(v_out.dtype)

    ss_pe = jnp.sum(jnp.where(lane < rope_dim, kpe * kpe, 0.0), axis=-1, keepdims=True)
    rot = kpe * kpeg_ref[...] * fk_ref[...]
    rot = rot + pltpu.roll(rot, rope_dim, 1)
    kf = jnp.where(lane >= LANES - 2 * rope_dim, rot, 0.0)

    k_raw = _dot(kvn, wk_ref[...])
    kg = kg_ref[...]
    for hh in range(n_heads):
        kh = k_raw[:, hh * LANES:(hh + 1) * LANES]
        ss = jnp.sum(kh * kh, axis=-1, keepdims=True) + ss_pe
        r = lax.rsqrt(ss * (1.0 / qk_dim) + EPS)
        k_out[:, hh * LANES:(hh + 1) * LANES] = ((kh * kg + kf) * r).astype(k_out.dtype)

    q_raw = _dot(qn, wq_ref[...])
    gf = qg_ref[...] * fq_ref[...]
    for hh in range(n_heads):
        qh = q_raw[:, hh * LANES:(hh + 1) * LANES]
        ss = jnp.sum(jnp.where(lane < qk_dim, qh * qh, 0.0), axis=-1, keepdims=True)
        r = lax.rsqrt(ss * (1.0 / qk_dim) + EPS)
        q_out[:, hh * LANES:(hh + 1) * LANES] = (qh * gf * r).astype(q_out.dtype)


def _swap_halves(a):
    n = a.shape[-1] // 2
    return jnp.concatenate([a[..., n:], a[..., :n]], axis=-1)


def mla_prep(h, ln_g, w_in, q_a_g, w_q_b, kv_a_g, w_kv_b, q_norm_g, k_norm_g, *, seq_pad, name):
    t, d = h.shape
    nh, nope, rope, vd = MLA_HEADS, MLA_NOPE, MLA_ROPE, MLA_V
    qk = nope + rope
    q_rank, kv_rank = q_a_g.shape[0], kv_a_g.shape[0]
    assert nope + 2 * rope == LANES
    tm = _pick_tile(seq_pad, (384, 256, 128))

    kpe_w = w_in[:, q_rank + kv_rank:]
    kpe_blk = jnp.concatenate([kpe_w, _swap_halves(kpe_w)] * 2, axis=1)
    w_in_l = jnp.concatenate([w_in[:, :q_rank + kv_rank], kpe_blk], axis=1).astype(BF16)

    wq = w_q_b.reshape(q_rank, nh, qk)
    wq_l = jnp.concatenate([wq, _swap_halves(wq[..., nope:])], axis=-1).reshape(q_rank, nh * LANES).astype(BF16)
    wkv = w_kv_b.reshape(kv_rank, nh, nope + vd)
    wk_l = _pad_cols(wkv[..., :nope], LANES).reshape(kv_rank, nh * LANES).astype(BF16)
    wv_l = wkv[..., nope:].reshape(kv_rank, nh * vd).T.astype(BF16)

    scale = qk ** -0.5 * LOG2_E
    qg_l = (jnp.concatenate([q_norm_g, _swap_halves(q_norm_g[nope:])]) * scale)[None]
    kg_l = _pad_cols(k_norm_g[:nope], LANES)[None]
    kpeg_l = jnp.concatenate([k_norm_g[nope:], _swap_halves(k_norm_g[nope:])] * 2)[None]

    inv = 1.0 / (ROPE_THETA ** (jnp.arange(0, rope, 2, dtype=F32) / rope))
    ang = jnp.arange(seq_pad, dtype=F32)[:, None] * inv[None, :]
    cos, sin = jnp.cos(ang), jnp.sin(ang)
    rot_f = jnp.concatenate([cos, cos, -sin, sin], axis=1)
    fq = jnp.concatenate([jnp.ones((seq_pad, nope), F32), rot_f], axis=1)
    fk = jnp.concatenate([rot_f, rot_f], axis=1)

    n_pos = seq_pad // tm
    kern = functools.partial(_mla_prep_kernel, n_heads=nh, q_rank=q_rank, kv_rank=kv_rank,
                             qk_dim=qk, rope_dim=rope)
    row = lambda i: (i, 0)
    pos = lambda i: (i % n_pos, 0)
    params = [ln_g[None], w_in_l, q_a_g[None], wq_l, kv_a_g[None], wk_l, wv_l, qg_l, kg_l, kpeg_l]
    return pl.pallas_call(
        kern,
        grid=(t // tm,),
        in_specs=[pl.BlockSpec((tm, d), row)] + [_full_spec(p) for p in params]
        + [pl.BlockSpec((tm, LANES), pos), pl.BlockSpec((tm, LANES), pos)],
        out_specs=[pl.BlockSpec((tm, nh * LANES), row), pl.BlockSpec((tm, nh * LANES), row),
                   pl.BlockSpec((nh * vd, tm), lambda i: (i // n_pos, i % n_pos))],
        out_shape=[jax.ShapeDtypeStruct((t, nh * LANES), BF16),
                   jax.ShapeDtypeStruct((t, nh * LANES), BF16),
                   jax.ShapeDtypeStruct((t // seq_pad * nh * vd, seq_pad), BF16)],
        compiler_params=_params(1),
        name=name,
    )(h, *params, fq, fk)


ATTN_UNROLL = 4
ATTN_TILE = 512


def _attn_kernel(q_ref, k_ref, vt_ref, o_ref, sa_ref, sb_ref, s0_ref, m_ref, l_ref, acc_ref,
                 *, tq, n_first, nq, vd):
    bufs = (sa_ref, sb_ref)
    heads = [slice(j * LANES, (j + 1) * LANES) for j in range(2)]

    def softmax_step(j, st, k_row0, nk):
        m_prev = m_ref[j]
        m_new = jnp.maximum(m_prev, jnp.max(st, axis=0, keepdims=True))
        alpha = jnp.exp2(m_prev - m_new)
        p = jnp.exp2(st - m_new)
        l_ref[j] = alpha * l_ref[j] + jnp.sum(p, axis=0, keepdims=True)
        m_ref[j] = m_new
        vt = vt_ref[j * vd:(j + 1) * vd, pl.ds(k_row0, nk)]
        acc_ref[j * vd:(j + 1) * vd, :] = (acc_ref[j * vd:(j + 1) * vd, :] * alpha
                                           + _dot(vt, p.astype(BF16)))

    if n_first:
        ki = lax.broadcasted_iota(jnp.int32, (n_first, n_first), 0)
        qj = lax.broadcasted_iota(jnp.int32, (n_first, n_first), 1)
        outs = []
        for j in range(2):
            st = _dot_nt(k_ref[0:n_first, heads[j]], q_ref[0:n_first, heads[j]])
            st = jnp.where(ki <= qj, st, NEG_BIG)
            p = jnp.exp2(st - jnp.max(st, axis=0, keepdims=True))
            pv = _dot(vt_ref[j * vd:(j + 1) * vd, 0:n_first], p.astype(BF16))
            outs.append(pv * (1.0 / jnp.sum(p, axis=0, keepdims=True)))
        o_ref[0:n_first, :] = jnp.concatenate(outs, axis=0).T.astype(o_ref.dtype)

    def q_block(qi, carry):
        q_row0 = pl.multiple_of(n_first + qi * tq, LANES)
        m_ref[...] = jnp.full(m_ref.shape, NEG_BIG, F32)
        l_ref[...] = jnp.zeros(l_ref.shape, F32)
        acc_ref[...] = jnp.zeros(acc_ref.shape, F32)

        def key_row(kb):
            return pl.multiple_of(n_first + kb * tq, LANES)

        def scores(kb, s_ref):
            for j in range(2):
                s_ref[j] = _dot_nt(k_ref[pl.ds(key_row(kb), tq), heads[j]],
                                   q_ref[pl.ds(q_row0, tq), heads[j]])

        def update(kb, s_ref, masked):
            for j in range(2):
                st = s_ref[j]
                if masked:
                    ki = lax.broadcasted_iota(jnp.int32, (tq, tq), 0)
                    qj = lax.broadcasted_iota(jnp.int32, (tq, tq), 1)
                    st = jnp.where(ki <= qj, st, NEG_BIG)
                softmax_step(j, st, key_row(kb), tq)

        if n_first:
            for j in range(2):
                s0_ref[j] = _dot_nt(k_ref[0:n_first, heads[j]], q_ref[pl.ds(q_row0, tq), heads[j]])
        scores(0, sa_ref)
        if n_first:
            for j in range(2):
                softmax_step(j, s0_ref[j], 0, n_first)

        def body(p, c):
            kb = ATTN_UNROLL * p
            for t in range(ATTN_UNROLL):
                scores(kb + t + 1, bufs[(t + 1) % 2])
                update(kb + t, bufs[t % 2], False)
            return c

        lax.fori_loop(0, lax.shift_right_logical(qi, ATTN_UNROLL.bit_length() - 1), body, 0)

        rem = jnp.bitwise_and(qi, ATTN_UNROLL - 1)
        k0 = qi - rem
        for r in range(min(ATTN_UNROLL, nq)):
            @pl.when(rem == r)
            def _(r=r):
                for t in range(r):
                    scores(k0 + t + 1, bufs[(t + 1) % 2])
                    update(k0 + t, bufs[t % 2], False)
                update(k0 + r, bufs[r % 2], True)

        inv_l = jnp.concatenate([jnp.broadcast_to(1.0 / l_ref[j], (vd, tq)) for j in range(2)], axis=0)
        o_ref[pl.ds(q_row0, tq), :] = (acc_ref[...] * inv_l).T.astype(o_ref.dtype)
        return carry

    lax.fori_loop(0, nq, q_block, 0)


def attention(q, k, vt, *, batch, seq_pad, name):
    t = q.shape[0]
    vd = MLA_V
    n_pairs = q.shape[1] // (2 * LANES)
    tq = min(ATTN_TILE, seq_pad)
    n_first, nq = seq_pad % tq, seq_pad // tq
    kern = functools.partial(_attn_kernel, tq=tq, n_first=n_first, nq=nq, vd=vd)
    return pl.pallas_call(
        kern,
        grid=(batch, n_pairs),
        in_specs=[pl.BlockSpec((seq_pad, 2 * LANES), lambda b, p: (b, p)),
                  pl.BlockSpec((seq_pad, 2 * LANES), lambda b, p: (b, p)),
                  pl.BlockSpec((2 * vd, seq_pad), lambda b, p: (b * n_pairs + p, 0))],
        out_specs=pl.BlockSpec((seq_pad, 2 * vd), lambda b, p: (b, p)),
        out_shape=jax.ShapeDtypeStruct((t, n_pairs * 2 * vd), BF16),
        scratch_shapes=[pltpu.VMEM((2, tq, tq), F32), pltpu.VMEM((2, tq, tq), F32),
                        pltpu.VMEM((2, max(n_first, 8), tq), F32),
                        pltpu.VMEM((2, 1, tq), F32), pltpu.VMEM((2, 1, tq), F32),
                        pltpu.VMEM((2 * vd, tq), F32)],
        compiler_params=_params(2),
        name=name,
    )(q, k, vt)


def mla_mixer(h, ln_g, w_in, q_a_g, w_q_b, kv_a_g, w_kv_b, q_norm_g, k_norm_g, *, batch, seq_pad, idx):
    q, k, vt = mla_prep(h, ln_g, w_in, q_a_g, w_q_b, kv_a_g, w_kv_b, q_norm_g, k_norm_g,
                        seq_pad=seq_pad, name=f"mla_prep_{idx}")
    return attention(q, k, vt, batch=batch, seq_pad=seq_pad, name=f"mla_attn_{idx}")


def kernel(x, meta_tokens, ln_mix, ln_mlp, ssd_w_in, ssd_conv_w, ssd_conv_b, ssd_dt_bias, ssd_a_log, ssd_d, ssd_norm, ssd_w_out, mla_w_in, mla_q_a_norm, mla_w_q_b, mla_kv_a_norm, mla_w_kv_b, mla_q_norm, mla_k_norm, mla_w_out, mlp_w_up, mlp_w_down):
    bsz, seq, d = x.shape
    n_meta = meta_tokens.shape[0]
    depth = ln_mix.shape[0]
    seq_all = n_meta + seq
    seq_pad = -(-seq_all // SSD_CHUNK) * SSD_CHUNK
    meta = jnp.broadcast_to(meta_tokens[None].astype(x.dtype), (bsz, n_meta, d))
    h = jnp.concatenate([meta, x, jnp.zeros((bsz, seq_pad - seq_all, d), x.dtype)], axis=1)
    h = h.reshape(bsz * seq_pad, d)
    for i in range(depth):
        j = i // 2
        if i % 2 == 0:
            a = ssd_mixer(h, ln_mix[i], ssd_w_in[j], ssd_conv_w[j], ssd_conv_b[j], ssd_dt_bias[j],
                          ssd_a_log[j], ssd_d[j], ssd_norm[j], seq_pad=seq_pad, idx=j)
            w_out = ssd_w_out[j]
        else:
            a = mla_mixer(h, ln_mix[i], mla_w_in[j], mla_q_a_norm[j], mla_w_q_b[j], mla_kv_a_norm[j],
                          mla_w_kv_b[j], mla_q_norm[j], mla_k_norm[j],
                          batch=bsz, seq_pad=seq_pad, idx=j)
            w_out = mla_w_out[j]
        h = proj_mlp_block(a, w_out.astype(BF16), h, ln_mlp[i][None], mlp_w_up[i].astype(BF16),
                           mlp_w_down[i].astype(BF16), name=f"proj_mlp_{i}")
    return h.reshape(bsz, seq_pad, d)[:, n_meta:seq_all]
```

```python
import functools

import jax
import jax.numpy as jnp
from jax import lax
from jax.experimental import pallas as pl
from jax.experimental.pallas import tpu as pltpu

F32 = jnp.float32
BF16 = jnp.bfloat16

EPS = 1e-6
ROPE_THETA = 10000.0
N_META = 16
SSD_HEAD_DIM = 64
SSD_STATE = 128
SSD_GROUPS = 8
SSD_CHUNK = 128
MLA_HEADS = 16
MLA_NOPE = 64
MLA_ROPE = 32
MLA_V = 64
MLA_Q_RANK = 384
MLA_KV_RANK = 256

LANES = 128
VMEM_LIMIT_BYTES = 56 * 1024 * 1024
NEG_BIG = -1e30
LOG2_E = 1.4426950408889634


def _params(n_axes):
    return pltpu.CompilerParams(
        dimension_semantics=("arbitrary",) * n_axes,
        vmem_limit_bytes=VMEM_LIMIT_BYTES)


def _pick_tile(n, candidates):
    for c in candidates:
        if n % c == 0:
            return c
    raise ValueError(f"no tile in {candidates} divides {n}")


def _full_spec(a):
    nd = a.ndim
    return pl.BlockSpec(a.shape, lambda *_: (0,) * nd, pipeline_mode=pl.Buffered(1))


def _rms(x, gain):
    ms = jnp.mean(x * x, axis=-1, keepdims=True)
    return x * lax.rsqrt(ms + EPS) * gain


def _silu(x):
    hx = 0.5 * x
    return hx + hx * jnp.tanh(hx)


def _dot(a, b):
    return jnp.dot(a, b, preferred_element_type=F32)


def _dot_nt(a, b):
    return lax.dot_general(a, b, (((1,), (1,)), ((), ())), preferred_element_type=F32)


def _proj_mlp_kernel(a_ref, wo_ref, h_ref, g_ref, wu_ref, wd_ref, o_ref, *, ff_chunk):
    x = h_ref[...] + _dot(a_ref[...], wo_ref[...])
    hn = _rms(x, g_ref[...]).astype(BF16)
    acc = x
    for c0 in range(0, wu_ref.shape[1], ff_chunk):
        u = jnp.maximum(_dot(hn, wu_ref[:, c0:c0 + ff_chunk]), 0.0)
        acc = acc + _dot((u * u).astype(BF16), wd_ref[c0:c0 + ff_chunk, :])
    o_ref[...] = acc


def proj_mlp_block(a, w_out, h, gain, w_up, w_down, *, name):
    t, d = h.shape
    k = a.shape[1]
    tm = _pick_tile(t, (512, 256, 128))
    kern = functools.partial(_proj_mlp_kernel, ff_chunk=1024)
    row = lambda i: (i, 0)
    return pl.pallas_call(
        kern,
        grid=(t // tm,),
        in_specs=[pl.BlockSpec((tm, k), row), _full_spec(w_out), pl.BlockSpec((tm, d), row),
                  _full_spec(gain), _full_spec(w_up), _full_spec(w_down)],
        out_specs=pl.BlockSpec((tm, d), row),
        out_shape=jax.ShapeDtypeStruct((t, d), F32),
        input_output_aliases={2: 0},
        compiler_params=_params(1),
        name=name,
    )(a, w_out, h, gain, w_up, w_down)


CONV_TAIL = 8


def _split3_bf16(x):
    p1 = x.astype(BF16)
    r1 = x - p1.astype(F32)
    p2 = r1.astype(BF16)
    p3 = (r1 - p2.astype(F32)).astype(BF16)
    return p1, p2, p3


def _ssd_kernel(h0_ref, ha_ref, hb_ref, g_ref, wz_ref, wx_ref, wdt_ref, cw_ref, cb_ref, dtb_ref,
                alog_ref, dsk_ref, ng_ref, o_ref,
                exta_ref, extb_ref, za_ref, zb_ref, dta_ref, dtb2_ref, hn_ref, xc_ref, st_ref,
                acs_ref, acst_ref, dts_ref,
                *, n_groups, hpg, hdim, n_state, chunk, chunks_per_batch, conv_chunk):
    i = pl.program_id(0)
    gw = hpg * hdim
    d_inner = n_groups * gw
    conv_dim = d_inner + 2 * n_groups * n_state
    conv_k = cw_ref.shape[0]
    slots = ((exta_ref, za_ref, dta_ref), (extb_ref, zb_ref, dtb2_ref))
    x_chunks = list(range(0, conv_dim, conv_chunk))

    row_i = lax.broadcasted_iota(jnp.int32, (chunk, chunk), 0)
    col_i = lax.broadcasted_iota(jnp.int32, (chunk, chunk), 1)
    causal = row_i >= col_i
    tri = jnp.where(causal, 1.0, 0.0).astype(BF16)
    lane = lax.broadcasted_iota(jnp.int32, (chunk, LANES), 1)
    first_half = lane < hdim
    row8 = lax.broadcasted_iota(jnp.int32, (1, CONV_TAIL, conv_chunk), 1)
    b_off = d_inner
    c_off = d_inner + n_groups * n_state

    def project_x(slot, c0):
        slots[slot][0][CONV_TAIL:CONV_TAIL + chunk, c0:c0 + conv_chunk] = _dot(
            hn_ref[...], wx_ref[:, c0:c0 + conv_chunk])

    def project_z(slot, g):
        slots[slot][1][:, g * gw:(g + 1) * gw] = _dot(
            hn_ref[...], wz_ref[:, g * gw:(g + 1) * gw]).astype(BF16)

    def project_dt(slot):
        slots[slot][2][...] = _dot(hn_ref[...], wdt_ref[...])

    def conv(slot, c0):
        ext = slots[slot][0]
        cols = slice(c0, c0 + conv_chunk)
        u = ext[CONV_TAIL:CONV_TAIL + chunk, cols]
        u3 = u.reshape(chunk // CONV_TAIL, CONV_TAIL, conv_chunk)
        prev = ext[0:CONV_TAIL, cols][None]
        acc = cb_ref[:, cols] + cw_ref[conv_k - 1:conv_k, cols] * u
        for k in range(conv_k - 1):
            s = conv_k - 1 - k
            rot = pltpu.roll(u3, s, 1)
            above = jnp.concatenate([pltpu.roll(prev, s, 1), rot[:-1]], axis=0)
            shifted = jnp.where(row8 < s, above, rot).reshape(chunk, conv_chunk)
            acc = acc + cw_ref[k:k + 1, cols] * shifted
        xc_ref[:, cols] = _silu(acc)

    def decay_setup(slot):
        dtr = slots[slot][2][...] + dtb_ref[...]
        dt = jnp.maximum(dtr, 0.0) + jnp.log(1.0 + jnp.exp(-jnp.abs(dtr)))
        dts_ref[...] = dt
        a_dt = dt * (-LOG2_E * jnp.exp(alog_ref[...]))
        p1, p2, p3 = _split3_bf16(a_dt)
        a_cs = _dot(tri, p1) + _dot(tri, p2) + _dot(tri, p3)
        acs_ref[...] = a_cs
        acst_ref[...] = a_cs.T

    def group(slot, g, row0):
        z_ref = slots[slot][1]
        xs_g = xc_ref[:, g * gw:(g + 1) * gw]
        b_f32 = xc_ref[:, b_off + g * n_state:b_off + (g + 1) * n_state]
        bt_g = b_f32.T.astype(BF16)
        c_g = xc_ref[:, c_off + g * n_state:c_off + (g + 1) * n_state].astype(BF16)
        cb = _dot(c_g, bt_g)
        prev_t = st_ref[g]
        y_off = _dot(c_g, prev_t.astype(BF16))
        y_parts, acs_parts, xdt_parts = [], [], []
        for pr in range(hpg // 2):
            heads = (g * hpg + 2 * pr, g * hpg + 2 * pr + 1)
            acs_p = jnp.where(first_half, acs_ref[:, heads[0]:heads[0] + 1],
                              acs_ref[:, heads[1]:heads[1] + 1])
            dt_p = jnp.where(first_half, dts_ref[:, heads[0]:heads[0] + 1],
                             dts_ref[:, heads[1]:heads[1] + 1])
            xdt_p = xs_g[:, pr * LANES:(pr + 1) * LANES] * dt_p
            xdt_bf = xdt_p.astype(BF16)
            prods = []
            for hh in heads:
                decay = jnp.exp2(jnp.where(causal, acs_ref[:, hh:hh + 1] - acst_ref[hh:hh + 1, :], -jnp.inf))
                prods.append(_dot((cb * decay).astype(BF16), xdt_bf))
            y_parts.append(jnp.where(first_half, prods[0], prods[1]))
            acs_parts.append(acs_p)
            xdt_parts.append(xdt_p)
        y_diag = jnp.concatenate(y_parts, axis=1)
        acs_x = jnp.concatenate(acs_parts, axis=1)
        xdt = jnp.concatenate(xdt_parts, axis=1)
        a_last = acs_x[chunk - 1:chunk, :]
        y = y_diag + y_off * jnp.exp2(acs_x) + xs_g * dsk_ref[:, g * gw:(g + 1) * gw]
        xdtd = (xdt * jnp.exp2(a_last - acs_x)).astype(BF16)
        st_ref[g] = prev_t * jnp.exp2(a_last) + _dot(bt_g, xdtd)
        zg = z_ref[:, g * gw:(g + 1) * gw].astype(F32)
        gated = y * _silu(zg)
        o_ref[row0:row0 + chunk, g * gw:(g + 1) * gw] = _rms(
            gated, ng_ref[:, g * gw:(g + 1) * gw]).astype(o_ref.dtype)

    def half(cur, nxt, h_next_ref, cidx, row0):
        ext_c, ext_n = slots[cur][0], slots[nxt][0]
        first = cidx % chunks_per_batch == 0

        @pl.when(first)
        def _():
            ext_c[0:CONV_TAIL, :] = jnp.zeros((CONV_TAIL, conv_dim), F32)
            st_ref[...] = jnp.zeros_like(st_ref)

        @pl.when(jnp.logical_not(first))
        def _():
            ext_c[0:CONV_TAIL, :] = ext_n[chunk:chunk + CONV_TAIL, :]

        hn_ref[...] = _rms(h_next_ref[...], g_ref[...]).astype(BF16)
        for c0 in x_chunks:
            project_x(nxt, c0)
            conv(cur, c0)
        project_dt(nxt)
        decay_setup(cur)
        for g in range(n_groups):
            project_z(nxt, g)
            group(cur, g, row0)

    @pl.when(i == 0)
    def _():
        hn_ref[...] = _rms(h0_ref[...], g_ref[...]).astype(BF16)
        for c0 in x_chunks:
            project_x(0, c0)
        project_dt(0)
        for g in range(n_groups):
            project_z(0, g)

    half(0, 1, ha_ref, 2 * i, 0)
    half(1, 0, hb_ref, 2 * i + 1, chunk)


def ssd_mixer(h, ln_g, w_in, conv_w, conv_b, dt_bias, a_log, d_skip, norm_g, *, seq_pad, idx):
    t, d = h.shape
    d_inner = norm_g.shape[0]
    n_heads = dt_bias.shape[0]
    conv_dim = conv_w.shape[1]
    n_state, hdim, chunk = SSD_STATE, SSD_HEAD_DIM, SSD_CHUNK
    n_groups = (conv_dim - d_inner) // (2 * n_state)
    hpg = d_inner // (n_groups * hdim)
    assert 2 * hdim == LANES and hpg % 2 == 0 and conv_w.shape[0] - 1 <= CONV_TAIL
    n_chunks = t // chunk
    assert n_chunks % 2 == 0 and seq_pad % chunk == 0

    w_z = w_in[:, :d_inner].astype(BF16)
    w_xbc = w_in[:, d_inner:d_inner + conv_dim].astype(BF16)
    w_dt = _pad_cols(w_in[:, d_inner + conv_dim:], LANES).astype(BF16)
    params = [ln_g[None], w_z, w_xbc, w_dt, conv_w, conv_b[None], _pad_cols(dt_bias[None], LANES),
              _pad_cols(a_log[None], LANES), jnp.repeat(d_skip, d_inner // n_heads)[None], norm_g[None]]

    kern = functools.partial(_ssd_kernel, n_groups=n_groups, hpg=hpg, hdim=hdim, n_state=n_state,
                             chunk=chunk, chunks_per_batch=seq_pad // chunk, conv_chunk=512)
    return pl.pallas_call(
        kern,
        grid=(n_chunks // 2,),
        in_specs=[pl.BlockSpec((chunk, d), lambda i: (0, 0)),
                  pl.BlockSpec((chunk, d), lambda i: (2 * i + 1, 0)),
                  pl.BlockSpec((chunk, d), lambda i: (jnp.minimum(2 * i + 2, n_chunks - 1), 0))]
        + [_full_spec(p) for p in params],
        out_specs=pl.BlockSpec((2 * chunk, d_inner), lambda i: (i, 0)),
        out_shape=jax.ShapeDtypeStruct((t, d_inner), BF16),
        scratch_shapes=[
            pltpu.VMEM((CONV_TAIL + chunk, conv_dim), F32),
            pltpu.VMEM((CONV_TAIL + chunk, conv_dim), F32),
            pltpu.VMEM((chunk, d_inner), BF16),
            pltpu.VMEM((chunk, d_inner), BF16),
            pltpu.VMEM((chunk, LANES), F32),
            pltpu.VMEM((chunk, LANES), F32),
            pltpu.VMEM((chunk, d), BF16),
            pltpu.VMEM((chunk, conv_dim), F32),
            pltpu.VMEM((n_groups, n_state, hpg * hdim), F32),
            pltpu.VMEM((chunk, LANES), F32),
            pltpu.VMEM((LANES, chunk), F32),
            pltpu.VMEM((chunk, LANES), F32),
        ],
        compiler_params=_params(1),
        name=f"ssd_mixer_{idx}",
    )(h, h, h, *params)


def _pad_cols(a, n):
    return jnp.pad(a, [(0, 0)] * (a.ndim - 1) + [(0, n - a.shape[-1])])


def _mla_prep_kernel(h_ref, g_ref, win_ref, qag_ref, wq_ref, kvag_ref, wk_ref, wv_ref,
                     qg_ref, kg_ref, kpeg_ref, fq_ref, fk_ref, q_out, k_out, v_out,
                     *, n_heads, q_rank, kv_rank, qk_dim, rope_dim):
    hn = _rms(h_ref[...], g_ref[...]).astype(BF16)
    lat = _dot(hn, win_ref[...])
    qn = _rms(lat[:, :q_rank], qag_ref[...]).astype(BF16)
    kvn = _rms(lat[:, q_rank:q_rank + kv_rank], kvag_ref[...]).astype(BF16)
    kpe = lat[:, q_rank + kv_rank:]
    tm = kpe.shape[0]
    lane = lax.broadcasted_iota(jnp.int32, (tm, LANES), 1)

    v_out[...] = _dot_nt(wv_ref[...], kvn).astype(v_out.dtype)

    ss_pe = jnp.sum(jnp.where(lane < rope_dim, kpe * kpe, 0.0), axis=-1, keepdims=True)
    rot = kpe * kpeg_ref[...] * fk_ref[...]
    rot = rot + pltpu.roll(rot, rope_dim, 1)
    kf = jnp.where(lane >= LANES - 2 * rope_dim, rot, 0.0)

    k_raw = _dot(kvn, wk_ref[...])
    kg = kg_ref[...]
    for hh in range(n_heads):
        kh = k_raw[:, hh * LANES:(hh + 1) * LANES]
        ss = jnp.sum(kh * kh, axis=-1, keepdims=True) + ss_pe
        r = lax.rsqrt(ss * (1.0 / qk_dim) + EPS)
        k_out[:, hh * LANES:(hh + 1) * LANES] = ((kh * kg + kf) * r).astype(k_out.dtype)

    gf = qg_ref[...] * fq_ref[...]
    sub = lax.broadcasted_iota(jnp.int32, (LANES, tm), 0)
    hb = 4
    for h0 in range(0, n_heads, hb):
        q_raw = _dot_nt(wq_ref[h0 * LANES:(h0 + hb) * LANES, :], qn)
        for hh in range(hb):
            qh = q_raw[hh * LANES:(hh + 1) * LANES, :]
            ss = jnp.sum(jnp.where(sub < qk_dim, qh * qh, 0.0), axis=0, keepdims=True)
            r = lax.rsqrt(ss * (1.0 / qk_dim) + EPS)
            q_out[(h0 + hh) * LANES:(h0 + hh + 1) * LANES, :] = (qh * gf * r).astype(q_out.dtype)


def _swap_halves(a):
    n = a.shape[-1] // 2
    return jnp.concatenate([a[..., n:], a[..., :n]], axis=-1)


def mla_prep(h, ln_g, w_in, q_a_g, w_q_b, kv_a_g, w_kv_b, q_norm_g, k_norm_g, *, seq_pad, name):
    t, d = h.shape
    nh, nope, rope, vd = MLA_HEADS, MLA_NOPE, MLA_ROPE, MLA_V
    qk = nope + rope
    q_rank, kv_rank = q_a_g.shape[0], kv_a_g.shape[0]
    assert nope + 2 * rope == LANES and nh % 4 == 0
    tm = _pick_tile(seq_pad, (384, 256, 128))

    kpe_w = w_in[:, q_rank + kv_rank:]
    kpe_blk = jnp.concatenate([kpe_w, _swap_halves(kpe_w)] * 2, axis=1)
    w_in_l = jnp.concatenate([w_in[:, :q_rank + kv_rank], kpe_blk], axis=1).astype(BF16)

    wq = w_q_b.reshape(q_rank, nh, qk)
    wq_l = jnp.concatenate([wq, _swap_halves(wq[..., nope:])], axis=-1).reshape(q_rank, nh * LANES).T.astype(BF16)
    wkv = w_kv_b.reshape(kv_rank, nh, nope + vd)
    wk_l = _pad_cols(wkv[..., :nope], LANES).reshape(kv_rank, nh * LANES).astype(BF16)
    wv_l = wkv[..., nope:].reshape(kv_rank, nh * vd).T.astype(BF16)

    scale = qk ** -0.5 * LOG2_E
    qg_l = (jnp.concatenate([q_norm_g, _swap_halves(q_norm_g[nope:])]) * scale)[:, None]
    kg_l = _pad_cols(k_norm_g[:nope], LANES)[None]
    kpeg_l = jnp.concatenate([k_norm_g[nope:], _swap_halves(k_norm_g[nope:])] * 2)[None]

    inv = 1.0 / (ROPE_THETA ** (jnp.arange(0, rope, 2, dtype=F32) / rope))
    ang = jnp.arange(seq_pad, dtype=F32)[:, None] * inv[None, :]
    cos, sin = jnp.cos(ang), jnp.sin(ang)
    rot_f = jnp.concatenate([cos, cos, -sin, sin], axis=1)
    fq = jnp.concatenate([jnp.ones((seq_pad, nope), F32), rot_f], axis=1)
    fk = jnp.concatenate([rot_f, rot_f], axis=1)

    n_pos = seq_pad // tm
    kern = functools.partial(_mla_prep_kernel, n_heads=nh, q_rank=q_rank, kv_rank=kv_rank,
                             qk_dim=qk, rope_dim=rope)
    row = lambda i: (i, 0)
    pos = lambda i: (i % n_pos, 0)
    by_batch = lambda i: (i // n_pos, i % n_pos)
    params = [ln_g[None], w_in_l, q_a_g[None], wq_l, kv_a_g[None], wk_l, wv_l, qg_l, kg_l, kpeg_l]
    return pl.pallas_call(
        kern,
        grid=(t // tm,),
        in_specs=[pl.BlockSpec((tm, d), row)] + [_full_spec(p) for p in params]
        + [pl.BlockSpec((LANES, tm), lambda i: (0, i % n_pos)), pl.BlockSpec((tm, LANES), pos)],
        out_specs=[pl.BlockSpec((nh * LANES, tm), by_batch), pl.BlockSpec((tm, nh * LANES), row),
                   pl.BlockSpec((nh * vd, tm), by_batch)],
        out_shape=[jax.ShapeDtypeStruct((t // seq_pad * nh * LANES, seq_pad), BF16),
                   jax.ShapeDtypeStruct((t, nh * LANES), BF16),
                   jax.ShapeDtypeStruct((t // seq_pad * nh * vd, seq_pad), BF16)],
        compiler_params=_params(1),
        name=name,
    )(h, *params, fq.T, fk)


ATTN_UNROLL = 4
ATTN_TILE = 512


def _attn_kernel(qt_ref, k_ref, vt_ref, o_ref, sa_ref, sb_ref, s0_ref, m_ref, l_ref, acc_ref,
                 *, tq, n_first, nq, vd):
    bufs = (sa_ref, sb_ref)
    heads = [slice(j * LANES, (j + 1) * LANES) for j in range(2)]

    def softmax_step(j, st, k_row0, nk):
        m_prev = m_ref[j]
        m_new = jnp.maximum(m_prev, jnp.max(st, axis=0, keepdims=True))
        alpha = jnp.exp2(m_prev - m_new)
        p = jnp.exp2(st - m_new)
        l_ref[j] = alpha * l_ref[j] + jnp.sum(p, axis=0, keepdims=True)
        m_ref[j] = m_new
        vt = vt_ref[j * vd:(j + 1) * vd, pl.ds(k_row0, nk)]
        acc_ref[j * vd:(j + 1) * vd, :] = (acc_ref[j * vd:(j + 1) * vd, :] * alpha
                                           + _dot(vt, p.astype(BF16)))

    if n_first:
        ki = lax.broadcasted_iota(jnp.int32, (n_first, n_first), 0)
        qj = lax.broadcasted_iota(jnp.int32, (n_first, n_first), 1)
        outs = []
        for j in range(2):
            st = _dot(k_ref[0:n_first, heads[j]], qt_ref[heads[j], 0:n_first])
            st = jnp.where(ki <= qj, st, NEG_BIG)
            p = jnp.exp2(st - jnp.max(st, axis=0, keepdims=True))
            pv = _dot(vt_ref[j * vd:(j + 1) * vd, 0:n_first], p.astype(BF16))
            outs.append(pv * (1.0 / jnp.sum(p, axis=0, keepdims=True)))
        o_ref[0:n_first, :] = jnp.concatenate(outs, axis=0).T.astype(o_ref.dtype)

    def q_block(qi, carry):
        q_row0 = pl.multiple_of(n_first + qi * tq, LANES)
        m_ref[...] = jnp.full(m_ref.shape, NEG_BIG, F32)
        l_ref[...] = jnp.zeros(l_ref.shape, F32)
        acc_ref[...] = jnp.zeros(acc_ref.shape, F32)

        def key_row(kb):
            return pl.multiple_of(n_first + kb * tq, LANES)

        def scores(kb, s_ref):
            for j in range(2):
                s_ref[j] = _dot(k_ref[pl.ds(key_row(kb), tq), heads[j]],
                                qt_ref[heads[j], pl.ds(q_row0, tq)])

        def update(kb, s_ref, masked):
            for j in range(2):
                st = s_ref[j]
                if masked:
                    ki = lax.broadcasted_iota(jnp.int32, (tq, tq), 0)
                    qj = lax.broadcasted_iota(jnp.int32, (tq, tq), 1)
                    st = jnp.where(ki <= qj, st, NEG_BIG)
                softmax_step(j, st, key_row(kb), tq)

        if n_first:
            for j in range(2):
                s0_ref[j] = _dot(k_ref[0:n_first, heads[j]], qt_ref[heads[j], pl.ds(q_row0, tq)])
        scores(0, sa_ref)
        if n_first:
            for j in range(2):
                softmax_step(j, s0_ref[j], 0, n_first)

        def body(p, c):
            kb = ATTN_UNROLL * p
            for t in range(ATTN_UNROLL):
                scores(kb + t + 1, bufs[(t + 1) % 2])
                update(kb + t, bufs[t % 2], False)
            return c

        lax.fori_loop(0, lax.shift_right_logical(qi, ATTN_UNROLL.bit_length() - 1), body, 0)

        rem = jnp.bitwise_and(qi, ATTN_UNROLL - 1)
        k0 = qi - rem
        for r in range(min(ATTN_UNROLL, nq)):
            @pl.when(rem == r)
            def _(r=r):
                for t in range(r):
                    scores(k0 + t + 1, bufs[(t + 1) % 2])
                    update(k0 + t, bufs[t % 2], False)
                update(k0 + r, bufs[r % 2], True)

        inv_l = jnp.concatenate([jnp.broadcast_to(1.0 / l_ref[j], (vd, tq)) for j in range(2)], axis=0)
        o_ref[pl.ds(q_row0, tq), :] = (acc_ref[...] * inv_l).T.astype(o_ref.dtype)
        return carry

    lax.fori_loop(0, nq, q_block, 0)


def attention(qt, k, vt, *, batch, seq_pad, name):
    t = k.shape[0]
    vd = MLA_V
    n_pairs = k.shape[1] // (2 * LANES)
    tq = min(ATTN_TILE, seq_pad)
    n_first, nq = seq_pad % tq, seq_pad // tq
    kern = functools.partial(_attn_kernel, tq=tq, n_first=n_first, nq=nq, vd=vd)
    return pl.pallas_call(
        kern,
        grid=(batch, n_pairs),
        in_specs=[pl.BlockSpec((2 * LANES, seq_pad), lambda b, p: (b * n_pairs + p, 0)),
                  pl.BlockSpec((seq_pad, 2 * LANES), lambda b, p: (b, p)),
                  pl.BlockSpec((2 * vd, seq_pad), lambda b, p: (b * n_pairs + p, 0))],
        out_specs=pl.BlockSpec((seq_pad, 2 * vd), lambda b, p: (b, p)),
        out_shape=jax.ShapeDtypeStruct((t, n_pairs * 2 * vd), BF16),
        scratch_shapes=[pltpu.VMEM((2, tq, tq), F32), pltpu.VMEM((2, tq, tq), F32),
                        pltpu.VMEM((2, max(n_first, 8), tq), F32),
                        pltpu.VMEM((2, 1, tq), F32), pltpu.VMEM((2, 1, tq), F32),
                        pltpu.VMEM((2 * vd, tq), F32)],
        compiler_params=_params(2),
        name=name,
    )(qt, k, vt)


def mla_mixer(h, ln_g, w_in, q_a_g, w_q_b, kv_a_g, w_kv_b, q_norm_g, k_norm_g, *, batch, seq_pad, idx):
    qt, k, vt = mla_prep(h, ln_g, w_in, q_a_g, w_q_b, kv_a_g, w_kv_b, q_norm_g, k_norm_g,
                         seq_pad=seq_pad, name=f"mla_prep_{idx}")
    return attention(qt, k, vt, batch=batch, seq_pad=seq_pad, name=f"mla_attn_{idx}")


def kernel(x, meta_tokens, ln_mix, ln_mlp, ssd_w_in, ssd_conv_w, ssd_conv_b, ssd_dt_bias, ssd_a_log, ssd_d, ssd_norm, ssd_w_out, mla_w_in, mla_q_a_norm, mla_w_q_b, mla_kv_a_norm, mla_w_kv_b, mla_q_norm, mla_k_norm, mla_w_out, mlp_w_up, mlp_w_down):
    bsz, seq, d = x.shape
    n_meta = meta_tokens.shape[0]
    depth = ln_mix.shape[0]
    seq_all = n_meta + seq
    seq_pad = -(-seq_all // SSD_CHUNK) * SSD_CHUNK
    meta = jnp.broadcast_to(meta_tokens[None].astype(x.dtype), (bsz, n_meta, d))
    h = jnp.concatenate([meta, x, jnp.zeros((bsz, seq_pad - seq_all, d), x.dtype)], axis=1)
    h = h.reshape(bsz * seq_pad, d)
    for i in range(depth):
        j = i // 2
        if i % 2 == 0:
            a = ssd_mixer(h, ln_mix[i], ssd_w_in[j], ssd_conv_w[j], ssd_conv_b[j], ssd_dt_bias[j],
                          ssd_a_log[j], ssd_d[j], ssd_norm[j], seq_pad=seq_pad, idx=j)
            w_out = ssd_w_out[j]
        else:
            a = mla_mixer(h, ln_mix[i], mla_w_in[j], mla_q_a_norm[j], mla_w_q_b[j], mla_kv_a_norm[j],
                          mla_w_kv_b[j], mla_q_norm[j], mla_k_norm[j],
                          batch=bsz, seq_pad=seq_pad, idx=j)
            w_out = mla_w_out[j]
        h = proj_mlp_block(a, w_out.astype(BF16), h, ln_mlp[i][None], mlp_w_up[i].astype(BF16),
                           mlp_w_down[i].astype(BF16), name=f"proj_mlp_{i}")
    return h.reshape(bsz, seq_pad, d)[:, n_meta:seq_all]
```

```python
import functools

import jax
import jax.numpy as jnp
from jax import lax
from jax.experimental import pallas as pl
from jax.experimental.pallas import tpu as pltpu

F32 = jnp.float32
BF16 = jnp.bfloat16

EPS = 1e-6
ROPE_THETA = 10000.0
N_META = 16
SSD_HEAD_DIM = 64
SSD_STATE = 128
SSD_GROUPS = 8
SSD_CHUNK = 128
MLA_HEADS = 16
MLA_NOPE = 64
MLA_ROPE = 32
MLA_V = 64
MLA_Q_RANK = 384
MLA_KV_RANK = 256

LANES = 128
VMEM_LIMIT_BYTES = 56 * 1024 * 1024
NEG_BIG = -1e30
LOG2_E = 1.4426950408889634


def _params(n_axes):
    return pltpu.CompilerParams(
        dimension_semantics=("arbitrary",) * n_axes,
        vmem_limit_bytes=VMEM_LIMIT_BYTES)


def _pick_tile(n, candidates):
    for c in candidates:
        if n % c == 0:
            return c
    raise ValueError(f"no tile in {candidates} divides {n}")


def _full_spec(a):
    nd = a.ndim
    return pl.BlockSpec(a.shape, lambda *_: (0,) * nd, pipeline_mode=pl.Buffered(1))


def _rms(x, gain):
    ms = jnp.mean(x * x, axis=-1, keepdims=True)
    return x * lax.rsqrt(ms + EPS) * gain


def _silu(x):
    hx = 0.5 * x
    return hx + hx * jnp.tanh(hx)


def _dot(a, b):
    return jnp.dot(a, b, preferred_element_type=F32)


def _dot_nt(a, b):
    return lax.dot_general(a, b, (((1,), (1,)), ((), ())), preferred_element_type=F32)


def _proj_mlp_kernel(a_ref, wo_ref, h_ref, g_ref, wu_ref, wd_ref, o_ref, *, ff_chunk):
    x = h_ref[...] + _dot(a_ref[...], wo_ref[...])
    hn = _rms(x, g_ref[...]).astype(BF16)
    acc = x
    for c0 in range(0, wu_ref.shape[1], ff_chunk):
        u = jnp.maximum(_dot(hn, wu_ref[:, c0:c0 + ff_chunk]), 0.0)
        acc = acc + _dot((u * u).astype(BF16), wd_ref[c0:c0 + ff_chunk, :])
    o_ref[...] = acc


def proj_mlp_block(a, w_out, h, gain, w_up, w_down, *, name):
    t, d = h.shape
    k = a.shape[1]
    tm = _pick_tile(t, (512, 256, 128))
    kern = functools.partial(_proj_mlp_kernel, ff_chunk=1024)
    row = lambda i: (i, 0)
    return pl.pallas_call(
        kern,
        grid=(t // tm,),
        in_specs=[pl.BlockSpec((tm, k), row), _full_spec(w_out), pl.BlockSpec((tm, d), row),
                  _full_spec(gain), _full_spec(w_up), _full_spec(w_down)],
        out_specs=pl.BlockSpec((tm, d), row),
        out_shape=jax.ShapeDtypeStruct((t, d), F32),
        input_output_aliases={2: 0},
        compiler_params=_params(1),
        name=name,
    )(a, w_out, h, gain, w_up, w_down)


CONV_TAIL = 8


def _split3_bf16(x):
    p1 = x.astype(BF16)
    r1 = x - p1.astype(F32)
    p2 = r1.astype(BF16)
    p3 = (r1 - p2.astype(F32)).astype(BF16)
    return p1, p2, p3


def _ssd_kernel(h0_ref, ha_ref, hb_ref, g_ref, wz_ref, wx_ref, wdt_ref, cw_ref, cb_ref, dtb_ref,
                alog_ref, dsk_ref, ng_ref, o_ref,
                exta_ref, extb_ref, za_ref, zb_ref, dta_ref, dtb2_ref, hn_ref, xc_ref, st_ref,
                acs_ref, acst_ref, dts_ref,
                *, n_groups, hpg, hdim, n_state, chunk, chunks_per_batch, conv_chunk):
    i = pl.program_id(0)
    gw = hpg * hdim
    d_inner = n_groups * gw
    conv_dim = d_inner + 2 * n_groups * n_state
    conv_k = cw_ref.shape[0]
    slots = ((exta_ref, za_ref, dta_ref), (extb_ref, zb_ref, dtb2_ref))
    x_chunks = list(range(0, conv_dim, conv_chunk))

    row_i = lax.broadcasted_iota(jnp.int32, (chunk, chunk), 0)
    col_i = lax.broadcasted_iota(jnp.int32, (chunk, chunk), 1)
    causal = row_i >= col_i
    tri = jnp.where(causal, 1.0, 0.0).astype(BF16)
    lane = lax.broadcasted_iota(jnp.int32, (chunk, LANES), 1)
    first_half = lane < hdim
    row8 = lax.broadcasted_iota(jnp.int32, (1, CONV_TAIL, conv_chunk), 1)
    b_off = d_inner
    c_off = d_inner + n_groups * n_state

    def project_x(slot, c0):
        slots[slot][0][CONV_TAIL:CONV_TAIL + chunk, c0:c0 + conv_chunk] = _dot(
            hn_ref[...], wx_ref[:, c0:c0 + conv_chunk])

    def project_z(slot, g):
        slots[slot][1][:, g * gw:(g + 1) * gw] = _dot(
            hn_ref[...], wz_ref[:, g * gw:(g + 1) * gw]).astype(BF16)

    def project_dt(slot):
        slots[slot][2][...] = _dot(hn_ref[...], wdt_ref[...])

    def conv(slot, c0):
        ext = slots[slot][0]
        cols = slice(c0, c0 + conv_chunk)
        u = ext[CONV_TAIL:CONV_TAIL + chunk, cols]
        u3 = u.reshape(chunk // CONV_TAIL, CONV_TAIL, conv_chunk)
        prev = ext[0:CONV_TAIL, cols][None]
        acc = cb_ref[:, cols] + cw_ref[conv_k - 1:conv_k, cols] * u
        for k in range(conv_k - 1):
            s = conv_k - 1 - k
            rot = pltpu.roll(u3, s, 1)
            above = jnp.concatenate([pltpu.roll(prev, s, 1), rot[:-1]], axis=0)
            shifted = jnp.where(row8 < s, above, rot).reshape(chunk, conv_chunk)
            acc = acc + cw_ref[k:k + 1, cols] * shifted
        xc_ref[:, cols] = _silu(acc)

    def decay_setup(slot):
        dtr = slots[slot][2][...] + dtb_ref[...]
        dt = jnp.maximum(dtr, 0.0) + jnp.log(1.0 + jnp.exp(-jnp.abs(dtr)))
        dts_ref[...] = dt
        a_dt = dt * (-LOG2_E * jnp.exp(alog_ref[...]))
        p1, p2, p3 = _split3_bf16(a_dt)
        a_cs = _dot(tri, p1) + _dot(tri, p2) + _dot(tri, p3)
        acs_ref[...] = a_cs
        acst_ref[...] = a_cs.T

    def group(slot, g, row0):
        z_ref = slots[slot][1]
        xs_g = xc_ref[:, g * gw:(g + 1) * gw]
        b_f32 = xc_ref[:, b_off + g * n_state:b_off + (g + 1) * n_state]
        bt_g = b_f32.T.astype(BF16)
        c_g = xc_ref[:, c_off + g * n_state:c_off + (g + 1) * n_state].astype(BF16)
        cb = _dot(c_g, bt_g)
        prev_t = st_ref[g]
        y_off = _dot(c_g, prev_t.astype(BF16))
        y_parts, acs_parts, xdt_parts = [], [], []
        for pr in range(hpg // 2):
            heads = (g * hpg + 2 * pr, g * hpg + 2 * pr + 1)
            acs_p = jnp.where(first_half, acs_ref[:, heads[0]:heads[0] + 1],
                              acs_ref[:, heads[1]:heads[1] + 1])
            dt_p = jnp.where(first_half, dts_ref[:, heads[0]:heads[0] + 1],
                             dts_ref[:, heads[1]:heads[1] + 1])
            xdt_p = xs_g[:, pr * LANES:(pr + 1) * LANES] * dt_p
            xdt_bf = xdt_p.astype(BF16)
            prods = []
            for hh in heads:
                decay = jnp.exp2(jnp.where(causal, acs_ref[:, hh:hh + 1] - acst_ref[hh:hh + 1, :], -jnp.inf))
                prods.append(_dot((cb * decay).astype(BF16), xdt_bf))
            y_parts.append(jnp.where(first_half, prods[0], prods[1]))
            acs_parts.append(acs_p)
            xdt_parts.append(xdt_p)
        y_diag = jnp.concatenate(y_parts, axis=1)
        acs_x = jnp.concatenate(acs_parts, axis=1)
        xdt = jnp.concatenate(xdt_parts, axis=1)
        a_last = acs_x[chunk - 1:chunk, :]
        y = y_diag + y_off * jnp.exp2(acs_x) + xs_g * dsk_ref[:, g * gw:(g + 1) * gw]
        xdtd = (xdt * jnp.exp2(a_last - acs_x)).astype(BF16)
        st_ref[g] = prev_t * jnp.exp2(a_last) + _dot(bt_g, xdtd)
        zg = z_ref[:, g * gw:(g + 1) * gw].astype(F32)
        gated = y * _silu(zg)
        o_ref[row0:row0 + chunk, g * gw:(g + 1) * gw] = _rms(
            gated, ng_ref[:, g * gw:(g + 1) * gw]).astype(o_ref.dtype)

    def half(cur, nxt, h_next_ref, cidx, row0):
        ext_c, ext_n = slots[cur][0], slots[nxt][0]
        first = cidx % chunks_per_batch == 0

        @pl.when(first)
        def _():
            ext_c[0:CONV_TAIL, :] = jnp.zeros((CONV_TAIL, conv_dim), F32)
            st_ref[...] = jnp.zeros_like(st_ref)

        @pl.when(jnp.logical_not(first))
        def _():
            ext_c[0:CONV_TAIL, :] = ext_n[chunk:chunk + CONV_TAIL, :]

        hn_ref[...] = _rms(h_next_ref[...], g_ref[...]).astype(BF16)
        for c0 in x_chunks:
            project_x(nxt, c0)
            conv(cur, c0)
        project_dt(nxt)
        decay_setup(cur)
        for g in range(n_groups):
            project_z(nxt, g)
            group(cur, g, row0)

    @pl.when(i == 0)
    def _():
        hn_ref[...] = _rms(h0_ref[...], g_ref[...]).astype(BF16)
        for c0 in x_chunks:
            project_x(0, c0)
        project_dt(0)
        for g in range(n_groups):
            project_z(0, g)

    half(0, 1, ha_ref, 2 * i, 0)
    half(1, 0, hb_ref, 2 * i + 1, chunk)


def ssd_mixer(h, ln_g, w_in, conv_w, conv_b, dt_bias, a_log, d_skip, norm_g, *, seq_pad, idx):
    t, d = h.shape
    d_inner = norm_g.shape[0]
    n_heads = dt_bias.shape[0]
    conv_dim = conv_w.shape[1]
    n_state, hdim, chunk = SSD_STATE, SSD_HEAD_DIM, SSD_CHUNK
    n_groups = (conv_dim - d_inner) // (2 * n_state)
    hpg = d_inner // (n_groups * hdim)
    assert 2 * hdim == LANES and hpg % 2 == 0 and conv_w.shape[0] - 1 <= CONV_TAIL
    n_chunks = t // chunk
    assert n_chunks % 2 == 0 and seq_pad % chunk == 0

    w_z = w_in[:, :d_inner].astype(BF16)
    w_xbc = w_in[:, d_inner:d_inner + conv_dim].astype(BF16)
    w_dt = _pad_cols(w_in[:, d_inner + conv_dim:], LANES).astype(BF16)
    params = [ln_g[None], w_z, w_xbc, w_dt, conv_w, conv_b[None], _pad_cols(dt_bias[None], LANES),
              _pad_cols(a_log[None], LANES), jnp.repeat(d_skip, d_inner // n_heads)[None], norm_g[None]]

    kern = functools.partial(_ssd_kernel, n_groups=n_groups, hpg=hpg, hdim=hdim, n_state=n_state,
                             chunk=chunk, chunks_per_batch=seq_pad // chunk, conv_chunk=512)
    return pl.pallas_call(
        kern,
        grid=(n_chunks // 2,),
        in_specs=[pl.BlockSpec((chunk, d), lambda i: (0, 0)),
                  pl.BlockSpec((chunk, d), lambda i: (2 * i + 1, 0)),
                  pl.BlockSpec((chunk, d), lambda i: (jnp.minimum(2 * i + 2, n_chunks - 1), 0))]
        + [_full_spec(p) for p in params],
        out_specs=pl.BlockSpec((2 * chunk, d_inner), lambda i: (i, 0)),
        out_shape=jax.ShapeDtypeStruct((t, d_inner), BF16),
        scratch_shapes=[
            pltpu.VMEM((CONV_TAIL + chunk, conv_dim), F32),
            pltpu.VMEM((CONV_TAIL + chunk, conv_dim), F32),
            pltpu.VMEM((chunk, d_inner), BF16),
            pltpu.VMEM((chunk, d_inner), BF16),
            pltpu.VMEM((chunk, LANES), F32),
            pltpu.VMEM((chunk, LANES), F32),
            pltpu.VMEM((chunk, d), BF16),
            pltpu.VMEM((chunk, conv_dim), F32),
            pltpu.VMEM((n_groups, n_state, hpg * hdim), F32),
            pltpu.VMEM((chunk, LANES), F32),
            pltpu.VMEM((LANES, chunk), F32),
            pltpu.VMEM((chunk, LANES), F32),
        ],
        compiler_params=_params(1),
        name=f"ssd_mixer_{idx}",
    )(h, h, h, *params)


def _pad_cols(a, n):
    return jnp.pad(a, [(0, 0)] * (a.ndim - 1) + [(0, n - a.shape[-1])])


def _mla_prep_kernel(h_ref, g_ref, win_ref, qag_ref, wq_ref, kvag_ref, wk_ref, wv_ref,
                     qg_ref, kg_ref, kpeg_ref, fq_ref, fk_ref, q_out, k_out, v_out,
                     *, n_heads, q_rank, kv_rank, qk_dim, rope_dim):
    hn = _rms(h_ref[...], g_ref[...]).astype(BF16)
    lat = _dot(hn, win_ref[...])
    qn = _rms(lat[:, :q_rank], qag_ref[...]).astype(BF16)
    kvn = _rms(lat[:, q_rank:q_rank + kv_rank], kvag_ref[...]).astype(BF16)
    kpe = lat[:, q_rank + kv_rank:]
    tm = kpe.shape[0]
    lane = lax.broadcasted_iota(jnp.int32, (tm, LANES), 1)

    v_out[...] = _dot_nt(wv_ref[...], kvn).astype(v_out.dtype)

    ss_pe = jnp.sum(jnp.where(lane < rope_dim, kpe * kpe, 0.0), axis=-1, keepdims=True)
    rot = kpe * kpeg_ref[...] * fk_ref[...]
    rot = rot + pltpu.roll(rot, rope_dim, 1)
    kf = jnp.where(lane >= LANES - 2 * rope_dim, rot, 0.0)

    k_raw = _dot(kvn, wk_ref[...])
    kg = kg_ref[...]
    for hh in range(n_heads):
        kh = k_raw[:, hh * LANES:(hh + 1) * LANES]
        ss = jnp.sum(kh * kh, axis=-1, keepdims=True) + ss_pe
        r = lax.rsqrt(ss * (1.0 / qk_dim) + EPS)
        k_out[:, hh * LANES:(hh + 1) * LANES] = ((kh * kg + kf) * r).astype(k_out.dtype)

    gf = qg_ref[...] * fq_ref[...]
    sub = lax.broadcasted_iota(jnp.int32, (LANES, tm), 0)
    hb = 4
    for h0 in range(0, n_heads, hb):
        q_raw = _dot_nt(wq_ref[h0 * LANES:(h0 + hb) * LANES, :], qn)
        for hh in range(hb):
            qh = q_raw[hh * LANES:(hh + 1) * LANES, :]
            ss = jnp.sum(jnp.where(sub < qk_dim, qh * qh, 0.0), axis=0, keepdims=True)
            r = lax.rsqrt(ss * (1.0 / qk_dim) + EPS)
            q_out[(h0 + hh) * LANES:(h0 + hh + 1) * LANES, :] = (qh * gf * r).astype(q_out.dtype)


def _swap_halves(a):
    n = a.shape[-1] // 2
    return jnp.concatenate([a[..., n:], a[..., :n]], axis=-1)


def mla_prep(h, ln_g, w_in, q_a_g, w_q_b, kv_a_g, w_kv_b, q_norm_g, k_norm_g, *, seq_pad, name):
    t, d = h.shape
    nh, nope, rope, vd = MLA_HEADS, MLA_NOPE, MLA_ROPE, MLA_V
    qk = nope + rope
    q_rank, kv_rank = q_a_g.shape[0], kv_a_g.shape[0]
    assert nope + 2 * rope == LANES and nh % 4 == 0
    tm = _pick_tile(seq_pad, (384, 256, 128))

    kpe_w = w_in[:, q_rank + kv_rank:]
    kpe_blk = jnp.concatenate([kpe_w, _swap_halves(kpe_w)] * 2, axis=1)
    w_in_l = jnp.concatenate([w_in[:, :q_rank + kv_rank], kpe_blk], axis=1).astype(BF16)

    wq = w_q_b.reshape(q_rank, nh, qk)
    wq_l = jnp.concatenate([wq, _swap_halves(wq[..., nope:])], axis=-1).reshape(q_rank, nh * LANES).T.astype(BF16)
    wkv = w_kv_b.reshape(kv_rank, nh, nope + vd)
    wk_l = _pad_cols(wkv[..., :nope], LANES).reshape(kv_rank, nh * LANES).astype(BF16)
    wv_l = wkv[..., nope:].reshape(kv_rank, nh * vd).T.astype(BF16)

    scale = qk ** -0.5 * LOG2_E
    qg_l = (jnp.concatenate([q_norm_g, _swap_halves(q_norm_g[nope:])]) * scale)[:, None]
    kg_l = _pad_cols(k_norm_g[:nope], LANES)[None]
    kpeg_l = jnp.concatenate([k_norm_g[nope:], _swap_halves(k_norm_g[nope:])] * 2)[None]

    inv = 1.0 / (ROPE_THETA ** (jnp.arange(0, rope, 2, dtype=F32) / rope))
    ang = jnp.arange(seq_pad, dtype=F32)[:, None] * inv[None, :]
    cos, sin = jnp.cos(ang), jnp.sin(ang)
    rot_f = jnp.concatenate([cos, cos, -sin, sin], axis=1)
    fq = jnp.concatenate([jnp.ones((seq_pad, nope), F32), rot_f], axis=1)
    fk = jnp.concatenate([rot_f, rot_f], axis=1)

    n_pos = seq_pad // tm
    kern = functools.partial(_mla_prep_kernel, n_heads=nh, q_rank=q_rank, kv_rank=kv_rank,
                             qk_dim=qk, rope_dim=rope)
    row = lambda i: (i, 0)
    pos = lambda i: (i % n_pos, 0)
    by_batch = lambda i: (i // n_pos, i % n_pos)
    params = [ln_g[None], w_in_l, q_a_g[None], wq_l, kv_a_g[None], wk_l, wv_l, qg_l, kg_l, kpeg_l]
    return pl.pallas_call(
        kern,
        grid=(t // tm,),
        in_specs=[pl.BlockSpec((tm, d), row)] + [_full_spec(p) for p in params]
        + [pl.BlockSpec((LANES, tm), lambda i: (0, i % n_pos)), pl.BlockSpec((tm, LANES), pos)],
        out_specs=[pl.BlockSpec((nh * LANES, tm), by_batch), pl.BlockSpec((tm, nh * LANES), row),
                   pl.BlockSpec((nh * vd, tm), by_batch)],
        out_shape=[jax.ShapeDtypeStruct((t // seq_pad * nh * LANES, seq_pad), BF16),
                   jax.ShapeDtypeStruct((t, nh * LANES), BF16),
                   jax.ShapeDtypeStruct((t // seq_pad * nh * vd, seq_pad), BF16)],
        compiler_params=_params(1),
        name=name,
    )(h, *params, fq.T, fk)


ATTN_UNROLL = 4
ATTN_TILE = 512


def _attn_kernel(qt_ref, k_ref, vt_ref, o_ref, sa_ref, sb_ref, s0_ref, m_ref, l_ref, acc_ref,
                 *, tq, tk, n_first, nq, vd):
    bufs = (sa_ref, sb_ref)
    kpq = tq // tk
    heads = [slice(j * LANES, (j + 1) * LANES) for j in range(2)]

    def softmax_step(j, st, k_row0, nk):
        m_prev = m_ref[j]
        m_new = jnp.maximum(m_prev, jnp.max(st, axis=0, keepdims=True))
        alpha = jnp.exp2(m_prev - m_new)
        p = jnp.exp2(st - m_new)
        l_ref[j] = alpha * l_ref[j] + jnp.sum(p, axis=0, keepdims=True)
        m_ref[j] = m_new
        vt = vt_ref[j * vd:(j + 1) * vd, pl.ds(k_row0, nk)]
        acc_ref[j * vd:(j + 1) * vd, :] = (acc_ref[j * vd:(j + 1) * vd, :] * alpha
                                           + _dot(vt, p.astype(BF16)))

    if n_first:
        ki = lax.broadcasted_iota(jnp.int32, (n_first, n_first), 0)
        qj = lax.broadcasted_iota(jnp.int32, (n_first, n_first), 1)
        outs = []
        for j in range(2):
            st = _dot(k_ref[0:n_first, heads[j]], qt_ref[heads[j], 0:n_first])
            st = jnp.where(ki <= qj, st, NEG_BIG)
            p = jnp.exp2(st - jnp.max(st, axis=0, keepdims=True))
            pv = _dot(vt_ref[j * vd:(j + 1) * vd, 0:n_first], p.astype(BF16))
            outs.append(pv * (1.0 / jnp.sum(p, axis=0, keepdims=True)))
        o_ref[0:n_first, :] = jnp.concatenate(outs, axis=0).T.astype(o_ref.dtype)

    def q_block(qi, carry):
        q_row0 = pl.multiple_of(n_first + qi * tq, LANES)
        m_ref[...] = jnp.full(m_ref.shape, NEG_BIG, F32)
        l_ref[...] = jnp.zeros(l_ref.shape, F32)
        acc_ref[...] = jnp.zeros(acc_ref.shape, F32)

        def key_row(kb):
            return pl.multiple_of(n_first + kb * tk, LANES)

        def scores(kb, s_ref):
            for j in range(2):
                s_ref[j] = _dot(k_ref[pl.ds(key_row(kb), tk), heads[j]],
                                qt_ref[heads[j], pl.ds(q_row0, tq)])

        def update(kb, s_ref, diag):
            for j in range(2):
                st = s_ref[j]
                if diag is not None:
                    ki = lax.broadcasted_iota(jnp.int32, (tk, tq), 0)
                    qj = lax.broadcasted_iota(jnp.int32, (tk, tq), 1)
                    st = jnp.where(ki + diag * tk <= qj, st, NEG_BIG)
                softmax_step(j, st, key_row(kb), tk)

        if n_first:
            for j in range(2):
                s0_ref[j] = _dot(k_ref[0:n_first, heads[j]], qt_ref[heads[j], pl.ds(q_row0, tq)])
        scores(0, sa_ref)
        if n_first:
            for j in range(2):
                softmax_step(j, s0_ref[j], 0, n_first)

        def body(p, c):
            kb = ATTN_UNROLL * p
            for t in range(ATTN_UNROLL):
                scores(kb + t + 1, bufs[(t + 1) % 2])
                update(kb + t, bufs[t % 2], None)
            return c

        n_full = kpq * qi
        lax.fori_loop(0, lax.shift_right_logical(n_full, ATTN_UNROLL.bit_length() - 1), body, 0)

        rem = jnp.bitwise_and(n_full, ATTN_UNROLL - 1)
        k0 = n_full - rem
        for r in range(0, ATTN_UNROLL, kpq):
            @pl.when(rem == r)
            def _(r=r):
                n_t = r + kpq
                for t in range(n_t):
                    if t + 1 < n_t:
                        scores(k0 + t + 1, bufs[(t + 1) % 2])
                    update(k0 + t, bufs[t % 2], None if t < r else t - r)

        inv_l = jnp.concatenate([jnp.broadcast_to(1.0 / l_ref[j], (vd, tq)) for j in range(2)], axis=0)
        o_ref[pl.ds(q_row0, tq), :] = (acc_ref[...] * inv_l).T.astype(o_ref.dtype)
        return carry

    lax.fori_loop(0, nq, q_block, 0)


def attention(qt, k, vt, *, batch, seq_pad, name):
    t = k.shape[0]
    vd = MLA_V
    n_pairs = k.shape[1] // (2 * LANES)
    tq = min(ATTN_TILE, seq_pad)
    n_first, nq = seq_pad % tq, seq_pad // tq
    tk = tq // 2 if tq % (2 * LANES) == 0 else tq
    kern = functools.partial(_attn_kernel, tq=tq, tk=tk, n_first=n_first, nq=nq, vd=vd)
    return pl.pallas_call(
        kern,
        grid=(batch, n_pairs),
        in_specs=[pl.BlockSpec((2 * LANES, seq_pad), lambda b, p: (b * n_pairs + p, 0)),
                  pl.BlockSpec((seq_pad, 2 * LANES), lambda b, p: (b, p)),
                  pl.BlockSpec((2 * vd, seq_pad), lambda b, p: (b * n_pairs + p, 0))],
        out_specs=pl.BlockSpec((seq_pad, 2 * vd), lambda b, p: (b, p)),
        out_shape=jax.ShapeDtypeStruct((t, n_pairs * 2 * vd), BF16),
        scratch_shapes=[pltpu.VMEM((2, tk, tq), F32), pltpu.VMEM((2, tk, tq), F32),
                        pltpu.VMEM((2, max(n_first, 8), tq), F32),
                        pltpu.VMEM((2, 1, tq), F32), pltpu.VMEM((2, 1, tq), F32),
                        pltpu.VMEM((2 * vd, tq), F32)],
        compiler_params=_params(2),
        name=name,
    )(qt, k, vt)


def mla_mixer(h, ln_g, w_in, q_a_g, w_q_b, kv_a_g, w_kv_b, q_norm_g, k_norm_g, *, batch, seq_pad, idx):
    qt, k, vt = mla_prep(h, ln_g, w_in, q_a_g, w_q_b, kv_a_g, w_kv_b, q_norm_g, k_norm_g,
                         seq_pad=seq_pad, name=f"mla_prep_{idx}")
    return attention(qt, k, vt, batch=batch, seq_pad=seq_pad, name=f"mla_attn_{idx}")


def kernel(x, meta_tokens, ln_mix, ln_mlp, ssd_w_in, ssd_conv_w, ssd_conv_b, ssd_dt_bias, ssd_a_log, ssd_d, ssd_norm, ssd_w_out, mla_w_in, mla_q_a_norm, mla_w_q_b, mla_kv_a_norm, mla_w_kv_b, mla_q_norm, mla_k_norm, mla_w_out, mlp_w_up, mlp_w_down):
    bsz, seq, d = x.shape
    n_meta = meta_tokens.shape[0]
    depth = ln_mix.shape[0]
    seq_all = n_meta + seq
    seq_pad = -(-seq_all // SSD_CHUNK) * SSD_CHUNK
    meta = jnp.broadcast_to(meta_tokens[None].astype(x.dtype), (bsz, n_meta, d))
    h = jnp.concatenate([meta, x, jnp.zeros((bsz, seq_pad - seq_all, d), x.dtype)], axis=1)
    h = h.reshape(bsz * seq_pad, d)
    for i in range(depth):
        j = i // 2
        if i % 2 == 0:
            a = ssd_mixer(h, ln_mix[i], ssd_w_in[j], ssd_conv_w[j], ssd_conv_b[j], ssd_dt_bias[j],
                          ssd_a_log[j], ssd_d[j], ssd_norm[j], seq_pad=seq_pad, idx=j)
            w_out = ssd_w_out[j]
        else:
            a = mla_mixer(h, ln_mix[i], mla_w_in[j], mla_q_a_norm[j], mla_w_q_b[j], mla_kv_a_norm[j],
                          mla_w_kv_b[j], mla_q_norm[j], mla_k_norm[j],
                          batch=bsz, seq_pad=seq_pad, idx=j)
            w_out = mla_w_out[j]
        h = proj_mlp_block(a, w_out.astype(BF16), h, ln_mlp[i][None], mlp_w_up[i].astype(BF16),
                           mlp_w_down[i].astype(BF16), name=f"proj_mlp_{i}")
    return h.reshape(bsz, seq_pad, d)[:, n_meta:seq_all]
```

```python
import functools

import jax
import jax.numpy as jnp
from jax import lax
from jax.experimental import pallas as pl
from jax.experimental.pallas import tpu as pltpu

F32 = jnp.float32
BF16 = jnp.bfloat16

EPS = 1e-6
ROPE_THETA = 10000.0
N_META = 16
SSD_HEAD_DIM = 64
SSD_STATE = 128
SSD_GROUPS = 8
SSD_CHUNK = 128
MLA_HEADS = 16
MLA_NOPE = 64
MLA_ROPE = 32
MLA_V = 64
MLA_Q_RANK = 384
MLA_KV_RANK = 256

LANES = 128
VMEM_LIMIT_BYTES = 56 * 1024 * 1024
NEG_BIG = -1e30
LOG2_E = 1.4426950408889634


def _params(n_axes):
    return pltpu.CompilerParams(
        dimension_semantics=("arbitrary",) * n_axes,
        vmem_limit_bytes=VMEM_LIMIT_BYTES)


def _pick_tile(n, candidates):
    for c in candidates:
        if n % c == 0:
            return c
    raise ValueError(f"no tile in {candidates} divides {n}")


def _full_spec(a):
    nd = a.ndim
    return pl.BlockSpec(a.shape, lambda *_: (0,) * nd, pipeline_mode=pl.Buffered(1))


def _rms(x, gain):
    ms = jnp.mean(x * x, axis=-1, keepdims=True)
    return x * lax.rsqrt(ms + EPS) * gain


def _silu(x):
    hx = 0.5 * x
    return hx + hx * jnp.tanh(hx)


def _dot(a, b):
    return jnp.dot(a, b, preferred_element_type=F32)


def _dot_nt(a, b):
    return lax.dot_general(a, b, (((1,), (1,)), ((), ())), preferred_element_type=F32)


def _proj_mlp_kernel(a_ref, wo_ref, h_ref, g_ref, wu_ref, wd_ref, o_ref, *, ff_chunk):
    x = h_ref[...] + _dot(a_ref[...], wo_ref[...])
    hn = _rms(x, g_ref[...]).astype(BF16)
    acc = x
    for c0 in range(0, wu_ref.shape[1], ff_chunk):
        u = jnp.maximum(_dot(hn, wu_ref[:, c0:c0 + ff_chunk]), 0.0)
        acc = acc + _dot((u * u).astype(BF16), wd_ref[c0:c0 + ff_chunk, :])
    o_ref[...] = acc


def proj_mlp_block(a, w_out, h, gain, w_up, w_down, *, name):
    t, d = h.shape
    k = a.shape[1]
    tm = _pick_tile(t, (512, 256, 128))
    kern = functools.partial(_proj_mlp_kernel, ff_chunk=1024)
    row = lambda i: (i, 0)
    return pl.pallas_call(
        kern,
        grid=(t // tm,),
        in_specs=[pl.BlockSpec((tm, k), row), _full_spec(w_out), pl.BlockSpec((tm, d), row),
                  _full_spec(gain), _full_spec(w_up), _full_spec(w_down)],
        out_specs=pl.BlockSpec((tm, d), row),
        out_shape=jax.ShapeDtypeStruct((t, d), F32),
        input_output_aliases={2: 0},
        compiler_params=_params(1),
        name=name,
    )(a, w_out, h, gain, w_up, w_down)


CONV_TAIL = 8


def _split3_bf16(x):
    p1 = x.astype(BF16)
    r1 = x - p1.astype(F32)
    p2 = r1.astype(BF16)
    p3 = (r1 - p2.astype(F32)).astype(BF16)
    return p1, p2, p3


def _ssd_kernel(h0_ref, ha_ref, hb_ref, g_ref, wz_ref, wx_ref, wdt_ref, cw_ref, cb_ref, dtb_ref,
                alog_ref, dsk_ref, ng_ref, o_ref,
                exta_ref, extb_ref, za_ref, zb_ref, dta_ref, dtb2_ref, hn_ref, xc_ref, st_ref,
                acs_ref, acst_ref, dts_ref,
                *, n_groups, hpg, hdim, n_state, chunk, chunks_per_batch, conv_chunk):
    i = pl.program_id(0)
    gw = hpg * hdim
    d_inner = n_groups * gw
    conv_dim = d_inner + 2 * n_groups * n_state
    conv_k = cw_ref.shape[0]
    slots = ((exta_ref, za_ref, dta_ref), (extb_ref, zb_ref, dtb2_ref))
    x_chunks = list(range(0, conv_dim, conv_chunk))

    row_i = lax.broadcasted_iota(jnp.int32, (chunk, chunk), 0)
    col_i = lax.broadcasted_iota(jnp.int32, (chunk, chunk), 1)
    causal = row_i >= col_i
    tri = jnp.where(causal, 1.0, 0.0).astype(BF16)
    lane = lax.broadcasted_iota(jnp.int32, (chunk, LANES), 1)
    first_half = lane < hdim
    row8 = lax.broadcasted_iota(jnp.int32, (1, CONV_TAIL, conv_chunk), 1)
    b_off = d_inner
    c_off = d_inner + n_groups * n_state

    def project_x(slot, c0):
        slots[slot][0][CONV_TAIL:CONV_TAIL + chunk, c0:c0 + conv_chunk] = _dot(
            hn_ref[...], wx_ref[:, c0:c0 + conv_chunk])

    def project_z(slot, g):
        slots[slot][1][:, g * gw:(g + 1) * gw] = _dot(
            hn_ref[...], wz_ref[:, g * gw:(g + 1) * gw]).astype(BF16)

    def project_dt(slot):
        slots[slot][2][...] = _dot(hn_ref[...], wdt_ref[...])

    def conv(slot, c0):
        ext = slots[slot][0]
        cols = slice(c0, c0 + conv_chunk)
        u = ext[CONV_TAIL:CONV_TAIL + chunk, cols]
        u3 = u.reshape(chunk // CONV_TAIL, CONV_TAIL, conv_chunk)
        prev = ext[0:CONV_TAIL, cols][None]
        acc = cb_ref[:, cols] + cw_ref[conv_k - 1:conv_k, cols] * u
        for k in range(conv_k - 1):
            s = conv_k - 1 - k
            rot = pltpu.roll(u3, s, 1)
            above = jnp.concatenate([pltpu.roll(prev, s, 1), rot[:-1]], axis=0)
            shifted = jnp.where(row8 < s, above, rot).reshape(chunk, conv_chunk)
            acc = acc + cw_ref[k:k + 1, cols] * shifted
        xc_ref[:, cols] = _silu(acc)

    def decay_setup(slot):
        dtr = slots[slot][2][...] + dtb_ref[...]
        dt = jnp.maximum(dtr, 0.0) + jnp.log(1.0 + jnp.exp(-jnp.abs(dtr)))
        dts_ref[...] = dt
        a_dt = dt * (-LOG2_E * jnp.exp(alog_ref[...]))
        p1, p2, p3 = _split3_bf16(a_dt)
        a_cs = _dot(tri, p1) + _dot(tri, p2) + _dot(tri, p3)
        acs_ref[...] = a_cs
        acst_ref[...] = a_cs.T

    def group(slot, g, row0):
        z_ref = slots[slot][1]
        xs_g = xc_ref[:, g * gw:(g + 1) * gw]
        b_f32 = xc_ref[:, b_off + g * n_state:b_off + (g + 1) * n_state]
        bt_g = b_f32.T.astype(BF16)
        c_g = xc_ref[:, c_off + g * n_state:c_off + (g + 1) * n_state].astype(BF16)
        cb = _dot(c_g, bt_g)
        prev_t = st_ref[g]
        y_off = _dot(c_g, prev_t.astype(BF16))
        y_parts, acs_parts, xdt_parts = [], [], []
        for pr in range(hpg // 2):
            heads = (g * hpg + 2 * pr, g * hpg + 2 * pr + 1)
            acs_p = jnp.where(first_half, acs_ref[:, heads[0]:heads[0] + 1],
                              acs_ref[:, heads[1]:heads[1] + 1])
            dt_p = jnp.where(first_half, dts_ref[:, heads[0]:heads[0] + 1],
                             dts_ref[:, heads[1]:heads[1] + 1])
            xdt_p = xs_g[:, pr * LANES:(pr + 1) * LANES] * dt_p
            xdt_bf = xdt_p.astype(BF16)
            prods = []
            for hh in heads:
                decay = jnp.exp2(jnp.where(causal, acs_ref[:, hh:hh + 1] - acst_ref[hh:hh + 1, :], -jnp.inf))
                prods.append(_dot((cb * decay).astype(BF16), xdt_bf))
            y_parts.append(jnp.where(first_half, prods[0], prods[1]))
            acs_parts.append(acs_p)
            xdt_parts.append(xdt_p)
        y_diag = jnp.concatenate(y_parts, axis=1)
        acs_x = jnp.concatenate(acs_parts, axis=1)
        xdt = jnp.concatenate(xdt_parts, axis=1)
        a_last = acs_x[chunk - 1:chunk, :]
        y = y_diag + y_off * jnp.exp2(acs_x) + xs_g * dsk_ref[:, g * gw:(g + 1) * gw]
        xdtd = (xdt * jnp.exp2(a_last - acs_x)).astype(BF16)
        st_ref[g] = prev_t * jnp.exp2(a_last) + _dot(bt_g, xdtd)
        zg = z_ref[:, g * gw:(g + 1) * gw].astype(F32)
        gated = y * _silu(zg)
        o_ref[row0:row0 + chunk, g * gw:(g + 1) * gw] = _rms(
            gated, ng_ref[:, g * gw:(g + 1) * gw]).astype(o_ref.dtype)

    def half(cur, nxt, h_next_ref, cidx, row0):
        ext_c, ext_n = slots[cur][0], slots[nxt][0]
        first = cidx % chunks_per_batch == 0

        @pl.when(first)
        def _():
            ext_c[0:CONV_TAIL, :] = jnp.zeros((CONV_TAIL, conv_dim), F32)
            st_ref[...] = jnp.zeros_like(st_ref)

        @pl.when(jnp.logical_not(first))
        def _():
            ext_c[0:CONV_TAIL, :] = ext_n[chunk:chunk + CONV_TAIL, :]

        hn_ref[...] = _rms(h_next_ref[...], g_ref[...]).astype(BF16)
        for c0 in x_chunks:
            project_x(nxt, c0)
            conv(cur, c0)
        project_dt(nxt)
        decay_setup(cur)
        for g in range(n_groups):
            project_z(nxt, g)
            group(cur, g, row0)

    @pl.when(i == 0)
    def _():
        hn_ref[...] = _rms(h0_ref[...], g_ref[...]).astype(BF16)
        for c0 in x_chunks:
            project_x(0, c0)
        project_dt(0)
        for g in range(n_groups):
            project_z(0, g)

    half(0, 1, ha_ref, 2 * i, 0)
    half(1, 0, hb_ref, 2 * i + 1, chunk)


def ssd_mixer(h, ln_g, w_in, conv_w, conv_b, dt_bias, a_log, d_skip, norm_g, *, seq_pad, idx):
    t, d = h.shape
    d_inner = norm_g.shape[0]
    n_heads = dt_bias.shape[0]
    conv_dim = conv_w.shape[1]
    n_state, hdim, chunk = SSD_STATE, SSD_HEAD_DIM, SSD_CHUNK
    n_groups = (conv_dim - d_inner) // (2 * n_state)
    hpg = d_inner // (n_groups * hdim)
    assert 2 * hdim == LANES and hpg % 2 == 0 and conv_w.shape[0] - 1 <= CONV_TAIL
    n_chunks = t // chunk
    assert n_chunks % 2 == 0 and seq_pad % chunk == 0

    w_z = w_in[:, :d_inner].astype(BF16)
    w_xbc = w_in[:, d_inner:d_inner + conv_dim].astype(BF16)
    w_dt = _pad_cols(w_in[:, d_inner + conv_dim:], LANES).astype(BF16)
    params = [ln_g[None], w_z, w_xbc, w_dt, conv_w, conv_b[None], _pad_cols(dt_bias[None], LANES),
              _pad_cols(a_log[None], LANES), jnp.repeat(d_skip, d_inner // n_heads)[None], norm_g[None]]

    kern = functools.partial(_ssd_kernel, n_groups=n_groups, hpg=hpg, hdim=hdim, n_state=n_state,
                             chunk=chunk, chunks_per_batch=seq_pad // chunk, conv_chunk=512)
    return pl.pallas_call(
        kern,
        grid=(n_chunks // 2,),
        in_specs=[pl.BlockSpec((chunk, d), lambda i: (0, 0)),
                  pl.BlockSpec((chunk, d), lambda i: (2 * i + 1, 0)),
                  pl.BlockSpec((chunk, d), lambda i: (jnp.minimum(2 * i + 2, n_chunks - 1), 0))]
        + [_full_spec(p) for p in params],
        out_specs=pl.BlockSpec((2 * chunk, d_inner), lambda i: (i, 0)),
        out_shape=jax.ShapeDtypeStruct((t, d_inner), BF16),
        scratch_shapes=[
            pltpu.VMEM((CONV_TAIL + chunk, conv_dim), F32),
            pltpu.VMEM((CONV_TAIL + chunk, conv_dim), F32),
            pltpu.VMEM((chunk, d_inner), BF16),
            pltpu.VMEM((chunk, d_inner), BF16),
            pltpu.VMEM((chunk, LANES), F32),
            pltpu.VMEM((chunk, LANES), F32),
            pltpu.VMEM((chunk, d), BF16),
            pltpu.VMEM((chunk, conv_dim), F32),
            pltpu.VMEM((n_groups, n_state, hpg * hdim), F32),
            pltpu.VMEM((chunk, LANES), F32),
            pltpu.VMEM((LANES, chunk), F32),
            pltpu.VMEM((chunk, LANES), F32),
        ],
        compiler_params=_params(1),
        name=f"ssd_mixer_{idx}",
    )(h, h, h, *params)


def _pad_cols(a, n):
    return jnp.pad(a, [(0, 0)] * (a.ndim - 1) + [(0, n - a.shape[-1])])


def _mla_prep_kernel(h_ref, g_ref, win_ref, qag_ref, wq_ref, kvag_ref, wk_ref, wv_ref,
                     qg_ref, kg_ref, kpeg_ref, fq_ref, fk_ref, q_out, k_out, v_out,
                     *, n_heads, q_rank, kv_rank, qk_dim, rope_dim):
    hn = _rms(h_ref[...], g_ref[...]).astype(BF16)
    lat = _dot(hn, win_ref[...])
    qn = _rms(lat[:, :q_rank], qag_ref[...]).astype(BF16)
    kvn = _rms(lat[:, q_rank:q_rank + kv_rank], kvag_ref[...]).astype(BF16)
    kpe = lat[:, q_rank + kv_rank:]
    tm = kpe.shape[0]
    lane = lax.broadcasted_iota(jnp.int32, (tm, LANES), 1)

    v_out[...] = _dot_nt(wv_ref[...], kvn).astype(v_out.dtype)

    ss_pe = jnp.sum(jnp.where(lane < rope_dim, kpe * kpe, 0.0), axis=-1, keepdims=True)
    rot = kpe * kpeg_ref[...] * fk_ref[...]
    rot = rot + pltpu.roll(rot, rope_dim, 1)
    kf = jnp.where(lane >= LANES - 2 * rope_dim, rot, 0.0)

    k_raw = _dot(kvn, wk_ref[...])
    kg = kg_ref[...]
    for hh in range(n_heads):
        kh = k_raw[:, hh * LANES:(hh + 1) * LANES]
        ss = jnp.sum(kh * kh, axis=-1, keepdims=True) + ss_pe
        r = lax.rsqrt(ss * (1.0 / qk_dim) + EPS)
        k_out[:, hh * LANES:(hh + 1) * LANES] = ((kh * kg + kf) * r).astype(k_out.dtype)

    gf = qg_ref[...] * fq_ref[...]
    sub = lax.broadcasted_iota(jnp.int32, (LANES, tm), 0)
    hb = 4
    for h0 in range(0, n_heads, hb):
        q_raw = _dot_nt(wq_ref[h0 * LANES:(h0 + hb) * LANES, :], qn)
        for hh in range(hb):
            qh = q_raw[hh * LANES:(hh + 1) * LANES, :]
            ss = jnp.sum(jnp.where(sub < qk_dim, qh * qh, 0.0), axis=0, keepdims=True)
            r = lax.rsqrt(ss * (1.0 / qk_dim) + EPS)
            q_out[(h0 + hh) * LANES:(h0 + hh + 1) * LANES, :] = (qh * gf * r).astype(q_out.dtype)


def _swap_halves(a):
    n = a.shape[-1] // 2
    return jnp.concatenate([a[..., n:], a[..., :n]], axis=-1)


def mla_prep(h, ln_g, w_in, q_a_g, w_q_b, kv_a_g, w_kv_b, q_norm_g, k_norm_g, *, seq_pad, name):
    t, d = h.shape
    nh, nope, rope, vd = MLA_HEADS, MLA_NOPE, MLA_ROPE, MLA_V
    qk = nope + rope
    q_rank, kv_rank = q_a_g.shape[0], kv_a_g.shape[0]
    assert nope + 2 * rope == LANES and nh % 4 == 0
    tm = _pick_tile(seq_pad, (384, 256, 128))

    kpe_w = w_in[:, q_rank + kv_rank:]
    kpe_blk = jnp.concatenate([kpe_w, _swap_halves(kpe_w)] * 2, axis=1)
    w_in_l = jnp.concatenate([w_in[:, :q_rank + kv_rank], kpe_blk], axis=1).astype(BF16)

    wq = w_q_b.reshape(q_rank, nh, qk)
    wq_l = jnp.concatenate([wq, _swap_halves(wq[..., nope:])], axis=-1).reshape(q_rank, nh * LANES).T.astype(BF16)
    wkv = w_kv_b.reshape(kv_rank, nh, nope + vd)
    wk_l = _pad_cols(wkv[..., :nope], LANES).reshape(kv_rank, nh * LANES).astype(BF16)
    wv_l = wkv[..., nope:].reshape(kv_rank, nh * vd).T.astype(BF16)

    scale = qk ** -0.5 * LOG2_E
    qg_l = (jnp.concatenate([q_norm_g, _swap_halves(q_norm_g[nope:])]) * scale)[:, None]
    kg_l = _pad_cols(k_norm_g[:nope], LANES)[None]
    kpeg_l = jnp.concatenate([k_norm_g[nope:], _swap_halves(k_norm_g[nope:])] * 2)[None]

    inv = 1.0 / (ROPE_THETA ** (jnp.arange(0, rope, 2, dtype=F32) / rope))
    ang = jnp.arange(seq_pad, dtype=F32)[:, None] * inv[None, :]
    cos, sin = jnp.cos(ang), jnp.sin(ang)
    rot_f = jnp.concatenate([cos, cos, -sin, sin], axis=1)
    fq = jnp.concatenate([jnp.ones((seq_pad, nope), F32), rot_f], axis=1)
    fk = jnp.concatenate([rot_f, rot_f], axis=1)

    n_pos = seq_pad // tm
    kern = functools.partial(_mla_prep_kernel, n_heads=nh, q_rank=q_rank, kv_rank=kv_rank,
                             qk_dim=qk, rope_dim=rope)
    row = lambda i: (i, 0)
    pos = lambda i: (i % n_pos, 0)
    by_batch = lambda i: (i // n_pos, i % n_pos)
    params = [ln_g[None], w_in_l, q_a_g[None], wq_l, kv_a_g[None], wk_l, wv_l, qg_l, kg_l, kpeg_l]
    return pl.pallas_call(
        kern,
        grid=(t // tm,),
        in_specs=[pl.BlockSpec((tm, d), row)] + [_full_spec(p) for p in params]
        + [pl.BlockSpec((LANES, tm), lambda i: (0, i % n_pos)), pl.BlockSpec((tm, LANES), pos)],
        out_specs=[pl.BlockSpec((nh * LANES, tm), by_batch), pl.BlockSpec((tm, nh * LANES), row),
                   pl.BlockSpec((nh * vd, tm), by_batch)],
        out_shape=[jax.ShapeDtypeStruct((t // seq_pad * nh * LANES, seq_pad), BF16),
                   jax.ShapeDtypeStruct((t, nh * LANES), BF16),
                   jax.ShapeDtypeStruct((t // seq_pad * nh * vd, seq_pad), BF16)],
        compiler_params=_params(1),
        name=name,
    )(h, *params, fq.T, fk)


ATTN_UNROLL = 8
ATTN_TILE = 512


def _attn_kernel(qt_ref, k_ref, vt_ref, o_ref, sa_ref, sb_ref, s0_ref, m_ref, l_ref, acc_ref,
                 *, tq, tk, n_first, nq, vd):
    bufs = (sa_ref, sb_ref)
    kpq = tq // tk
    heads = [slice(j * LANES, (j + 1) * LANES) for j in range(2)]

    def softmax_step(j, st, k_row0, nk):
        m_prev = m_ref[j]
        m_new = jnp.maximum(m_prev, jnp.max(st, axis=0, keepdims=True))
        alpha = jnp.exp2(m_prev - m_new)
        p = jnp.exp2(st - m_new)
        l_ref[j] = alpha * l_ref[j] + jnp.sum(p, axis=0, keepdims=True)
        m_ref[j] = m_new
        vt = vt_ref[j * vd:(j + 1) * vd, pl.ds(k_row0, nk)]
        acc_ref[j * vd:(j + 1) * vd, :] = (acc_ref[j * vd:(j + 1) * vd, :] * alpha
                                           + _dot(vt, p.astype(BF16)))

    if n_first:
        ki = lax.broadcasted_iota(jnp.int32, (n_first, n_first), 0)
        qj = lax.broadcasted_iota(jnp.int32, (n_first, n_first), 1)
        outs = []
        for j in range(2):
            st = _dot(k_ref[0:n_first, heads[j]], qt_ref[heads[j], 0:n_first])
            st = jnp.where(ki <= qj, st, NEG_BIG)
            p = jnp.exp2(st - jnp.max(st, axis=0, keepdims=True))
            pv = _dot(vt_ref[j * vd:(j + 1) * vd, 0:n_first], p.astype(BF16))
            outs.append(pv * (1.0 / jnp.sum(p, axis=0, keepdims=True)))
        o_ref[0:n_first, :] = jnp.concatenate(outs, axis=0).T.astype(o_ref.dtype)

    def q_block(qi, carry):
        q_row0 = pl.multiple_of(n_first + qi * tq, LANES)
        m_ref[...] = jnp.full(m_ref.shape, NEG_BIG, F32)
        l_ref[...] = jnp.zeros(l_ref.shape, F32)
        acc_ref[...] = jnp.zeros(acc_ref.shape, F32)

        def key_row(kb):
            return pl.multiple_of(n_first + kb * tk, LANES)

        def scores(kb, s_ref):
            for j in range(2):
                s_ref[j] = _dot(k_ref[pl.ds(key_row(kb), tk), heads[j]],
                                qt_ref[heads[j], pl.ds(q_row0, tq)])

        def update(kb, s_ref, diag):
            for j in range(2):
                st = s_ref[j]
                if diag is not None:
                    ki = lax.broadcasted_iota(jnp.int32, (tk, tq), 0)
                    qj = lax.broadcasted_iota(jnp.int32, (tk, tq), 1)
                    st = jnp.where(ki + diag * tk <= qj, st, NEG_BIG)
                softmax_step(j, st, key_row(kb), tk)

        if n_first:
            for j in range(2):
                s0_ref[j] = _dot(k_ref[0:n_first, heads[j]], qt_ref[heads[j], pl.ds(q_row0, tq)])
        scores(0, sa_ref)
        if n_first:
            for j in range(2):
                softmax_step(j, s0_ref[j], 0, n_first)

        def body(p, c):
            kb = ATTN_UNROLL * p
            for t in range(ATTN_UNROLL):
                scores(kb + t + 1, bufs[(t + 1) % 2])
                update(kb + t, bufs[t % 2], None)
            return c

        n_full = kpq * qi
        lax.fori_loop(0, lax.shift_right_logical(n_full, ATTN_UNROLL.bit_length() - 1), body, 0)

        rem = jnp.bitwise_and(n_full, ATTN_UNROLL - 1)
        k0 = n_full - rem
        for r in range(0, ATTN_UNROLL, kpq):
            @pl.when(rem == r)
            def _(r=r):
                n_t = r + kpq
                for t in range(n_t):
                    if t + 1 < n_t:
                        scores(k0 + t + 1, bufs[(t + 1) % 2])
                    update(k0 + t, bufs[t % 2], None if t < r else t - r)

        inv_l = jnp.concatenate([jnp.broadcast_to(1.0 / l_ref[j], (vd, tq)) for j in range(2)], axis=0)
        o_ref[pl.ds(q_row0, tq), :] = (acc_ref[...] * inv_l).T.astype(o_ref.dtype)
        return carry

    lax.fori_loop(0, nq, q_block, 0)


def attention(qt, k, vt, *, batch, seq_pad, name):
    t = k.shape[0]
    vd = MLA_V
    n_pairs = k.shape[1] // (2 * LANES)
    tq = min(ATTN_TILE, seq_pad)
    n_first, nq = seq_pad % tq, seq_pad // tq
    tk = tq // 2 if tq % (2 * LANES) == 0 else tq
    kern = functools.partial(_attn_kernel, tq=tq, tk=tk, n_first=n_first, nq=nq, vd=vd)
    return pl.pallas_call(
        kern,
        grid=(batch, n_pairs),
        in_specs=[pl.BlockSpec((2 * LANES, seq_pad), lambda b, p: (b * n_pairs + p, 0)),
                  pl.BlockSpec((seq_pad, 2 * LANES), lambda b, p: (b, p)),
                  pl.BlockSpec((2 * vd, seq_pad), lambda b, p: (b * n_pairs + p, 0))],
        out_specs=pl.BlockSpec((seq_pad, 2 * vd), lambda b, p: (b, p)),
        out_shape=jax.ShapeDtypeStruct((t, n_pairs * 2 * vd), BF16),
        scratch_shapes=[pltpu.VMEM((2, tk, tq), F32), pltpu.VMEM((2, tk, tq), F32),
                        pltpu.VMEM((2, max(n_first, 8), tq), F32),
                        pltpu.VMEM((2, 1, tq), F32), pltpu.VMEM((2, 1, tq), F32),
                        pltpu.VMEM((2 * vd, tq), F32)],
        compiler_params=_params(2),
        name=name,
    )(qt, k, vt)


def mla_mixer(h, ln_g, w_in, q_a_g, w_q_b, kv_a_g, w_kv_b, q_norm_g, k_norm_g, *, batch, seq_pad, idx):
    qt, k, vt = mla_prep(h, ln_g, w_in, q_a_g, w_q_b, kv_a_g, w_kv_b, q_norm_g, k_norm_g,
                         seq_pad=seq_pad, name=f"mla_prep_{idx}")
    return attention(qt, k, vt, batch=batch, seq_pad=seq_pad, name=f"mla_attn_{idx}")


def kernel(x, meta_tokens, ln_mix, ln_mlp, ssd_w_in, ssd_conv_w, ssd_conv_b, ssd_dt_bias, ssd_a_log, ssd_d, ssd_norm, ssd_w_out, mla_w_in, mla_q_a_norm, mla_w_q_b, mla_kv_a_norm, mla_w_kv_b, mla_q_norm, mla_k_norm, mla_w_out, mlp_w_up, mlp_w_down):
    bsz, seq, d = x.shape
    n_meta = meta_tokens.shape[0]
    depth = ln_mix.shape[0]
    seq_all = n_meta + seq
    seq_pad = -(-seq_all // SSD_CHUNK) * SSD_CHUNK
    meta = jnp.broadcast_to(meta_tokens[None].astype(x.dtype), (bsz, n_meta, d))
    h = jnp.concatenate([meta, x, jnp.zeros((bsz, seq_pad - seq_all, d), x.dtype)], axis=1)
    h = h.reshape(bsz * seq_pad, d)
    for i in range(depth):
        j = i // 2
        if i % 2 == 0:
            a = ssd_mixer(h, ln_mix[i], ssd_w_in[j], ssd_conv_w[j], ssd_conv_b[j], ssd_dt_bias[j],
                          ssd_a_log[j], ssd_d[j], ssd_norm[j], seq_pad=seq_pad, idx=j)
            w_out = ssd_w_out[j]
        else:
            a = mla_mixer(h, ln_mix[i], mla_w_in[j], mla_q_a_norm[j], mla_w_q_b[j], mla_kv_a_norm[j],
                          mla_w_kv_b[j], mla_q_norm[j], mla_k_norm[j],
                          batch=bsz, seq_pad=seq_pad, idx=j)
            w_out = mla_w_out[j]
        h = proj_mlp_block(a, w_out.astype(BF16), h, ln_mlp[i][None], mlp_w_up[i].astype(BF16),
                           mlp_w_down[i].astype(BF16), name=f"proj_mlp_{i}")
    return h.reshape(bsz, seq_pad, d)[:, n_meta:seq_all]
```
